```python
import math
import jax, jax.numpy as jnp
from jax import lax
import numpy as np

D_MODEL = 2048
BATCH = 2
SEQ = 4096
DEPTH = 2

CHUNK = 64
QBLK = 128
HEAD_DIM = 128
N_HEADS = D_MODEL // HEAD_DIM
N_KV = N_HEADS // 4
GROUP = N_HEADS // N_KV
Q_DIM = N_HEADS * HEAD_DIM
KV_DIM = N_KV * HEAD_DIM
IDX_HEADS = 16
IDX_DIM = 128
IDX_ROPE = 64
IDXQ_DIM = IDX_HEADS * IDX_DIM
IN_A = Q_DIM + 2 * KV_DIM + IDXQ_DIM + IDX_DIM + IDX_HEADS
TOPK_MAX = 256
ROPE_THETA = 10000.0
CONV_W = 31
D_FF = 4 * D_MODEL
PLE_DIM = 256
MAX_OFFSET = 4096
N_A = (DEPTH + 1) // 2
N_B = DEPTH // 2
RMS_EPS = 1e-6
LN_EPS = 1e-5

kernel_name = "hybrid_dsa_conformer_conv_encoder"


def rmsnorm(x, g):
    xf = x.astype(jnp.float32)
    y = xf * lax.rsqrt(jnp.mean(xf * xf, axis=-1, keepdims=True) + RMS_EPS)
    return (y * g.astype(jnp.float32)).astype(x.dtype)


def layernorm(x, g, b):
    xf = x.astype(jnp.float32)
    mu = jnp.mean(xf, axis=-1, keepdims=True)
    xc = xf - mu
    y = xc * lax.rsqrt(jnp.mean(xc * xc, axis=-1, keepdims=True) + LN_EPS)
    return (y * g.astype(jnp.float32) + b.astype(jnp.float32)).astype(x.dtype)


def rope(x, positions, rot_dim):
    half = rot_dim // 2
    inv = ROPE_THETA ** (-jnp.arange(half, dtype=jnp.float32) * (2.0 / rot_dim))
    ang = positions.astype(jnp.float32)[..., None] * inv
    ang = ang.reshape(ang.shape[:2] + (1,) * (x.ndim - 3) + (half,))
    cos, sin = jnp.cos(ang), jnp.sin(ang)
    xr = x[..., :rot_dim].astype(jnp.float32)
    x1, x2 = xr[..., :half], xr[..., half:]
    rot = jnp.concatenate([x1 * cos - x2 * sin, x2 * cos + x1 * sin], axis=-1)
    return jnp.concatenate([rot.astype(x.dtype), x[..., rot_dim:]], axis=-1)


def dsa_mixer(hn, positions, w_in, w_out, kidx_g, kidx_b):
    B, S, _ = hn.shape
    k_top = min(TOPK_MAX, S // 4)
    proj = hn @ w_in
    o0 = Q_DIM
    o1 = o0 + KV_DIM
    o2 = o1 + KV_DIM
    o3 = o2 + IDXQ_DIM
    o4 = o3 + IDX_DIM
    q = proj[..., :o0].reshape(B, S, N_HEADS, HEAD_DIM)
    k = proj[..., o0:o1].reshape(B, S, N_KV, HEAD_DIM)
    v = proj[..., o1:o2].reshape(B, S, N_KV, HEAD_DIM)
    qi = proj[..., o2:o3].reshape(B, S, IDX_HEADS, IDX_DIM)
    ki = proj[..., o3:o4]
    wi = proj[..., o4:].astype(jnp.float32) * (IDX_HEADS ** -0.5 * IDX_DIM ** -0.5)

    q = rope(q, positions, HEAD_DIM)
    k = rope(k, positions, HEAD_DIM)
    ki = layernorm(ki, kidx_g, kidx_b)
    qi = rope(qi, positions, IDX_ROPE)
    ki = rope(ki, positions, IDX_ROPE)

    chunk_id = jnp.arange(S) // CHUNK
    neg = jnp.finfo(jnp.float32).min
    scale = HEAD_DIM ** -0.5

    def block(bi):
        t0 = bi * QBLK
        qb = lax.dynamic_slice_in_dim(q, t0, QBLK, axis=1)
        qib = lax.dynamic_slice_in_dim(qi, t0, QBLK, axis=1)
        wib = lax.dynamic_slice_in_dim(wi, t0, QBLK, axis=1)
        q_chunk = (t0 + jnp.arange(QBLK)) // CHUNK
        dots = jnp.einsum('bqhd,bsd->bqhs', qib, ki)
        score = jnp.einsum('bqhs,bqh->bqs', jax.nn.relu(dots).astype(jnp.float32), wib)
        admissible = chunk_id[None, :] <= q_chunk[:, None]
        score = jnp.where(admissible[None], score, neg)
        _, idx = lax.top_k(score, k_top)
        valid = (idx // CHUNK) <= q_chunk[None, :, None]
        ksel = jax.vmap(lambda kb, ib: kb[ib])(k, idx)
        vsel = jax.vmap(lambda vb, ib: vb[ib])(v, idx)
        qg = qb.reshape(B, QBLK, N_KV, GROUP, HEAD_DIM)
        logits = jnp.einsum('bqgrd,bqkgd->bqgrk', qg, ksel).astype(jnp.float32) * scale
        logits = jnp.where(valid[:, :, None, None, :], logits, neg)
        probs = jax.nn.softmax(logits, axis=-1).astype(vsel.dtype)
        o = jnp.einsum('bqgrk,bqkgd->bqgrd', probs, vsel)
        return o.reshape(B, QBLK, Q_DIM)

    out = lax.map(block, jnp.arange(S // QBLK))
    out = jnp.transpose(out, (1, 0, 2, 3)).reshape(B, S, Q_DIM)
    return out @ w_out


def conformer_conv(hn, w_pw1, b_pw1, w_dw, b_dw, ln_g, ln_b, w_pw2, b_pw2):
    D = hn.shape[-1]
    u = hn @ w_pw1 + b_pw1
    u = u[..., :D] * jax.nn.sigmoid(u[..., D:])
    u = lax.conv_general_dilated(
        u, w_dw[:, None, :], window_strides=(1,), padding=[(CONV_W - 1, 0)],
        dimension_numbers=('NWC', 'WIO', 'NWC'), feature_group_count=D) + b_dw
    u = jax.nn.silu(layernorm(u, ln_g, ln_b))
    return u @ w_pw2 + b_pw2


def sq_relu_mlp(hn, w1, w2):
    a = jax.nn.relu(hn @ w1)
    return (a * a) @ w2


def setup_inputs(seed: int = 0) -> dict:
    key = jax.random.key(seed)
    ks = jax.random.split(key, 24)
    f32 = jnp.float32
    nrm = lambda k, shape, s: jax.random.normal(k, shape, f32) * s
    x = jax.random.normal(ks[0], (BATCH, SEQ, D_MODEL), f32)
    p = jax.random.normal(ks[1], (DEPTH, BATCH, SEQ, PLE_DIM), f32)
    offset = jax.random.randint(ks[2], (BATCH, 1), 0, MAX_OFFSET)
    positions = (offset + jnp.arange(SEQ)[None, :]).astype(jnp.int32)
    return {
        "x": x,
        "p": p,
        "positions": positions,
        "norm_mix_g": 1.0 + nrm(ks[3], (DEPTH, D_MODEL), 0.02),
        "norm_mlp_g": 1.0 + nrm(ks[4], (DEPTH, D_MODEL), 0.02),
        "final_g": 1.0 + nrm(ks[5], (D_MODEL,), 0.02),
        "a_w_in": nrm(ks[6], (N_A, D_MODEL, IN_A), D_MODEL ** -0.5),
        "a_w_out": nrm(ks[7], (N_A, Q_DIM, D_MODEL), Q_DIM ** -0.5),
        "a_kidx_g": 1.0 + nrm(ks[8], (N_A, IDX_DIM), 0.02),
        "a_kidx_b": nrm(ks[9], (N_A, IDX_DIM), 0.02),
        "b_w_pw1": nrm(ks[10], (N_B, D_MODEL, 2 * D_MODEL), D_MODEL ** -0.5),
        "b_b_pw1": nrm(ks[11], (N_B, 2 * D_MODEL), 0.02),
        "b_w_dw": nrm(ks[12], (N_B, CONV_W, D_MODEL), CONV_W ** -0.5),
        "b_b_dw": nrm(ks[13], (N_B, D_MODEL), 0.02),
        "b_ln_g": 1.0 + nrm(ks[14], (N_B, D_MODEL), 0.02),
        "b_ln_b": nrm(ks[15], (N_B, D_MODEL), 0.02),
        "b_w_pw2": nrm(ks[16], (N_B, D_MODEL, D_MODEL), D_MODEL ** -0.5),
        "b_b_pw2": nrm(ks[17], (N_B, D_MODEL), 0.02),
        "mlp_w1": nrm(ks[18], (DEPTH, D_MODEL, D_FF), D_MODEL ** -0.5),
        "mlp_w2": nrm(ks[19], (DEPTH, D_FF, D_MODEL), D_FF ** -0.5),
        "ple_w_proj": nrm(ks[20], (DEPTH, PLE_DIM, D_MODEL), PLE_DIM ** -0.5),
        "ple_w_gate": nrm(ks[21], (DEPTH, D_MODEL, D_MODEL), D_MODEL ** -0.5),
    }


def reference(x, p, positions, norm_mix_g, norm_mlp_g, final_g,
              a_w_in, a_w_out, a_kidx_g, a_kidx_b,
              b_w_pw1, b_b_pw1, b_w_dw, b_b_dw, b_ln_g, b_ln_b, b_w_pw2, b_b_pw2,
              mlp_w1, mlp_w2, ple_w_proj, ple_w_gate):
    h = x
    for i in range(DEPTH):
        hn = rmsnorm(h, norm_mix_g[i])
        j = i // 2
        if i % 2 == 0:
            h = h + dsa_mixer(hn, positions, a_w_in[j], a_w_out[j], a_kidx_g[j], a_kidx_b[j])
        else:
            h = h + conformer_conv(hn, b_w_pw1[j], b_b_pw1[j], b_w_dw[j], b_b_dw[j],
                                   b_ln_g[j], b_ln_b[j], b_w_pw2[j], b_b_pw2[j])
        h = h + sq_relu_mlp(rmsnorm(h, norm_mlp_g[i]), mlp_w1[i], mlp_w2[i])
        h = h + jax.nn.sigmoid(h @ ple_w_gate[i]) * (p[i] @ ple_w_proj[i])
    return rmsnorm(h, final_g)
```

```python
import functools

import numpy as np
import jax
import jax.numpy as jnp
from jax import lax
from jax.experimental import pallas as pl
from jax.experimental.pallas import tpu as pltpu

F32 = jnp.float32
BF16 = jnp.bfloat16
I32 = jnp.int32

CHUNK = 64
QBLK = 128
HEAD_DIM = 128
GROUP = 4
IDX_HEADS = 16
IDX_DIM = 128
IDX_ROPE = 64
TOPK_MAX = 256
ROPE_THETA = 10000.0
CONV_W = 31
RMS_EPS = 1e-6
LN_EPS = 1e-5

LANES = 128
V7X_VMEM_BYTES = 64 * 1024 * 1024
VMEM_CAP = V7X_VMEM_BYTES - 8 * 1024 * 1024

INT_MIN = -2 ** 31
MASK_NEG = -1e30
NT_DIMS = (((1,), (1,)), ((), ()))


def _params(ndims, vmem_estimate):
    limit = min(int(vmem_estimate * 1.25) + (4 << 20), VMEM_CAP)
    return pltpu.CompilerParams(dimension_semantics=("arbitrary",) * ndims,
                                vmem_limit_bytes=limit)


def _rmsnorm_kernel(x_ref, g_ref, o_ref):
    x = x_ref[...]
    ms = jnp.mean(x * x, axis=-1, keepdims=True)
    o_ref[...] = (x * lax.rsqrt(ms + RMS_EPS) * g_ref[...]).astype(o_ref.dtype)


def _rmsnorm(x, g, out_dtype, tm=512):
    T, D = x.shape
    return pl.pallas_call(
        _rmsnorm_kernel,
        out_shape=jax.ShapeDtypeStruct((T, D), out_dtype),
        grid=(T // tm,),
        in_specs=[pl.BlockSpec((tm, D), lambda i: (i, 0)),
                  pl.BlockSpec((1, D), lambda i: (0, 0))],
        out_specs=pl.BlockSpec((tm, D), lambda i: (i, 0)),
        compiler_params=_params(1, 4 * tm * D * 4),
        name="rmsnorm",
    )(x, g.reshape(1, D))


def _rope_tables_kernel(pos_ref, inv_ref, sgn_ref, c128_ref, s128_ref, c64_ref, s64a_ref, s64b_ref):
    pos = pos_ref[...].astype(F32)
    ang = pos * inv_ref[0:1, :]
    c128_ref[...] = jnp.cos(ang)
    s128_ref[...] = jnp.sin(ang) * sgn_ref[0:1, :]
    ang = pos * inv_ref[1:2, :]
    c64_ref[...] = jnp.cos(ang)
    sn = jnp.sin(ang)
    s64a_ref[...] = sn * sgn_ref[1:2, :]
    s64b_ref[...] = sn * sgn_ref[2:3, :]


def _rope_tables(pos, tm=1024):
    T = pos.shape[0]
    half = HEAD_DIM // 2
    inv_h = ROPE_THETA ** (-jnp.arange(half, dtype=F32) * (2.0 / HEAD_DIM))
    half_i = IDX_ROPE // 2
    inv_i = ROPE_THETA ** (-jnp.arange(half_i, dtype=F32) * (2.0 / IDX_ROPE))
    inv = jnp.zeros((8, LANES), F32)
    inv = inv.at[0].set(jnp.concatenate([inv_h, inv_h]))
    inv = inv.at[1, :IDX_ROPE].set(jnp.concatenate([inv_i, inv_i]))
    sgn = np.zeros((8, LANES), np.float32)
    sgn[0, :half] = -1.0
    sgn[0, half:] = 1.0
    sgn[1, half_i:IDX_ROPE] = 1.0
    sgn[2, :half_i] = -1.0
    tab = jax.ShapeDtypeStruct((T, LANES), F32)
    row = pl.BlockSpec((tm, LANES), lambda i: (i, 0))
    cst = pl.BlockSpec((8, LANES), lambda i: (0, 0))
    return pl.pallas_call(
        _rope_tables_kernel,
        out_shape=(tab,) * 5,
        grid=(T // tm,),
        in_specs=[pl.BlockSpec((tm, 1), lambda i: (i, 0)), cst, cst],
        out_specs=(row,) * 5,
        compiler_params=_params(1, 16 * tm * LANES * 4),
        name="rope_tables",
    )(pos, inv, jnp.asarray(sgn))


def _proj_kernel(a_ref, w_ref, *rest, rope, head_major):
    o_ref = rest[-1]
    y = jnp.dot(a_ref[...], w_ref[...], preferred_element_type=F32)
    tm, tn = y.shape
    if rope == 0:
        o_ref[...] = y.astype(o_ref.dtype)
        return
    tabs = [r[...] for r in rest[:-1]]
    for h in range(tn // HEAD_DIM):
        yh = y[:, h * HEAD_DIM:(h + 1) * HEAD_DIM]
        if rope == HEAD_DIM:
            c, s = tabs
            out = yh * c + pltpu.roll(yh, HEAD_DIM // 2, 1) * s
        else:
            c, sa, sb = tabs
            out = (yh * c + pltpu.roll(yh, IDX_ROPE // 2, 1) * sa
                   + pltpu.roll(yh, HEAD_DIM - IDX_ROPE // 2, 1) * sb)
        out = out.astype(o_ref.dtype)
        if head_major:
            o_ref[:, h] = out.reshape(tm // QBLK, QBLK, HEAD_DIM)
        else:
            o_ref[:, h * HEAD_DIM:(h + 1) * HEAD_DIM] = out


def _proj(a, w, tabs, rope, head_major, tm=1024, tn=512):
    T, K = a.shape
    N = w.shape[1]
    tn = min(tn, N)
    nh = tn // HEAD_DIM
    if head_major:
        out_shape = jax.ShapeDtypeStruct((T // QBLK, N // HEAD_DIM, QBLK, HEAD_DIM), BF16)
        out_spec = pl.BlockSpec((tm // QBLK, nh, QBLK, HEAD_DIM), lambda i, j: (i, j, 0, 0))
    else:
        out_shape = jax.ShapeDtypeStruct((T, N), BF16)
        out_spec = pl.BlockSpec((tm, tn), lambda i, j: (i, j))
    tab_spec = pl.BlockSpec((tm, LANES), lambda i, j: (i, 0))
    vmem = 2 * (tm * K * 2 + K * tn * 2 + tm * tn * 2 + len(tabs) * tm * LANES * 4) + 3 * tm * tn * 4
    return pl.pallas_call(
        functools.partial(_proj_kernel, rope=rope, head_major=head_major),
        out_shape=out_shape,
        grid=(T // tm, N // tn),
        in_specs=[pl.BlockSpec((tm, K), lambda i, j: (i, 0)),
                  pl.BlockSpec((K, tn), lambda i, j: (0, j))] + [tab_spec] * len(tabs),
        out_specs=out_spec,
        compiler_params=_params(2, vmem),
        name=f"proj_rope{rope}_{'hm' if head_major else 'flat'}",
    )(a, w, *tabs)


def _ki_kernel(a_ref, wk_ref, wwt_ref, g_ref, b_ref, c_ref, sa_ref, sb_ref, ki_ref, wt_ref, *, w_scale):
    a = a_ref[...]
    y = jnp.dot(a, wk_ref[...], preferred_element_type=F32)
    mu = jnp.mean(y, axis=-1, keepdims=True)
    yc = y - mu
    var = jnp.mean(yc * yc, axis=-1, keepdims=True)
    yn = yc * lax.rsqrt(var + LN_EPS) * g_ref[...] + b_ref[...]
    out = (yn * c_ref[...] + pltpu.roll(yn, IDX_ROPE // 2, 1) * sa_ref[...]
           + pltpu.roll(yn, IDX_DIM - IDX_ROPE // 2, 1) * sb_ref[...])
    ki_ref[...] = out.astype(ki_ref.dtype)
    wt = lax.dot_general(wwt_ref[...], a, NT_DIMS, preferred_element_type=F32)
    wt_ref[...] = wt * w_scale


def _ki_proj(a, wk, wwt, g, b, tabs, tm=1024):
    T, K = a.shape
    nw = wwt.shape[0]
    tab_spec = pl.BlockSpec((tm, LANES), lambda i: (i, 0))
    vec_spec = pl.BlockSpec((1, IDX_DIM), lambda i: (0, 0))
    w_scale = IDX_HEADS ** -0.5 * IDX_DIM ** -0.5
    return pl.pallas_call(
        functools.partial(_ki_kernel, w_scale=w_scale),
        out_shape=(jax.ShapeDtypeStruct((T, IDX_DIM), BF16), jax.ShapeDtypeStruct((nw, T), F32)),
        grid=(T // tm,),
        in_specs=[pl.BlockSpec((tm, K), lambda i: (i, 0)),
                  pl.BlockSpec((K, IDX_DIM), lambda i: (0, 0)),
                  pl.BlockSpec((nw, K), lambda i: (0, 0)),
                  vec_spec, vec_spec, tab_spec, tab_spec, tab_spec],
        out_specs=(pl.BlockSpec((tm, IDX_DIM), lambda i: (i, 0)),
                   pl.BlockSpec((nw, tm), lambda i: (0, i))),
        compiler_params=_params(1, 2 * (tm * K * 2 + K * IDX_DIM * 2 + 8 * tm * LANES * 4)),
        name="ki_proj",
    )(a, wk, wwt, g.reshape(1, IDX_DIM), b.reshape(1, IDX_DIM), *tabs)


IDX_ROWS = 256


def _sortable_key(x):
    b = lax.bitcast_convert_type(x, I32)
    return b ^ ((b >> 31) & 0x7FFFFFFF)


def _dsa_kernel(q_ref, qi_ref, wt_ref, ki_ref, k_ref, v_ref, o_ref,
                key_ref, bias_ref, m_ref, l_ref, acc_ref, *, k_top, scale):
    i = pl.program_id(1)
    seq = ki_ref.shape[0]
    n_heads = q_ref.shape[1]
    R = IDX_ROWS
    nch = (i * QBLK + QBLK + R - 1) // R
    sub = R // QBLK

    q_glob = i * QBLK + lax.broadcasted_iota(I32, (1, LANES), 1)
    q_chunk = q_glob // CHUNK

    qi_all = qi_ref[0].reshape(IDX_HEADS * QBLK, IDX_DIM)

    def idx_body(c, carry):
        r0 = pl.multiple_of(c * R, R)
        kib = ki_ref[pl.ds(r0, R), :]
        d = lax.dot_general(kib, qi_all, NT_DIMS, preferred_element_type=F32)
        acc = jnp.zeros((R, LANES), F32)
        for h in range(IDX_HEADS):
            acc = acc + jnp.maximum(d[:, h * QBLK:(h + 1) * QBLK], 0.0) * wt_ref[h:h + 1, :]
        row = r0 + lax.broadcasted_iota(I32, (R, LANES), 0)
        adm = (row // CHUNK) <= q_chunk
        key_ref[pl.ds(r0, R), :] = jnp.where(adm, _sortable_key(acc), INT_MIN)
        return carry

    lax.fori_loop(0, nch, idx_body, 0)

    def count(pred):
        def body(c, cnt):
            r0 = pl.multiple_of(c * R, R)
            blk = key_ref[pl.ds(r0, R), :]
            row = r0 + lax.broadcasted_iota(I32, (R, LANES), 0)
            return cnt + jnp.sum(pred(blk, row).reshape(R // 8, 8, LANES), axis=0)
        cnt = lax.fori_loop(0, nch, body, jnp.zeros((8, LANES), I32))
        return jnp.sum(cnt, axis=0, keepdims=True)

    def thr_pass(b, lo):
        cand = lo + jnp.left_shift(jnp.int32(1), 31 - b)
        cnt = count(lambda key, row: jnp.where(key >= cand, 1, 0))
        return jnp.where(cnt >= k_top, cand, lo)

    thr = lax.fori_loop(0, 32, thr_pass, jnp.full((1, LANES), INT_MIN, I32))

    c_gt = count(lambda key, row: jnp.where(key > thr, 1, 0))
    c_eq = count(lambda key, row: jnp.where(key == thr, 1, 0))
    need = k_top - c_gt
    n_bits = int(seq).bit_length()

    def tie_rows():
        def tie_pass(b, jlim):
            cand = jlim + jnp.left_shift(jnp.int32(1), n_bits - 1 - b)
            cnt = count(lambda key, row: jnp.where(key == thr, jnp.where(row < cand, 1, 0), 0))
            return jnp.where(cnt <= need, cand, jlim)
        return lax.fori_loop(0, n_bits, tie_pass, jnp.zeros((1, LANES), I32))

    has_excess_ties = jnp.max(jnp.where(c_eq > need, 1, 0)) > 0
    jlim = lax.cond(has_excess_ties, tie_rows, lambda: jnp.full((1, LANES), 2 * seq, I32))

    def bias_body(c, carry):
        r0 = pl.multiple_of(c * QBLK, QBLK)
        key = key_ref[pl.ds(r0, QBLK), :]
        row = r0 + lax.broadcasted_iota(I32, (QBLK, LANES), 0)
        tie_ok = jnp.where(row < jlim, 0.0, MASK_NEG)
        val = jnp.where(key > thr, 0.0, jnp.where(key == thr, tie_ok, MASK_NEG))
        val = jnp.where((row // CHUNK) <= q_chunk, val, MASK_NEG)
        bias_ref[c] = val.T
        return carry

    lax.fori_loop(0, nch * sub, bias_body, 0)

    rows = GROUP * QBLK
    for g in range(n_heads // GROUP):
        qg = q_ref[0, g * GROUP:(g + 1) * GROUP].reshape(rows, HEAD_DIM)
        m_ref[...] = jnp.full((rows, 1), -3e38, F32)
        l_ref[...] = jnp.zeros((rows, 1), F32)
        acc_ref[...] = jnp.zeros((rows, HEAD_DIM), F32)

        def kv_body(c, carry):
            r0 = pl.multiple_of(c * R, R)
            kblk = k_ref[pl.ds(r0, R), g * HEAD_DIM:(g + 1) * HEAD_DIM]
            vblk = v_ref[pl.ds(r0, R), g * HEAD_DIM:(g + 1) * HEAD_DIM]
            s = lax.dot_general(qg, kblk, NT_DIMS, preferred_element_type=F32) * scale
            bias = jnp.concatenate([bias_ref[c * sub + t] for t in range(sub)], axis=1)
            s = (s.reshape(GROUP, QBLK, R) + bias[None]).reshape(rows, R)
            m_old = m_ref[...]
            m_new = jnp.maximum(m_old, jnp.max(s, axis=-1, keepdims=True))
            alpha = jnp.exp(m_old - m_new)
            p = jnp.exp(s - m_new)
            l_ref[...] = alpha * l_ref[...] + jnp.sum(p, axis=-1, keepdims=True)
            acc_ref[...] = alpha * acc_ref[...] + jnp.dot(p.astype(BF16), vblk,
                                                         preferred_element_type=F32)
            m_ref[...] = m_new
            return carry

        lax.fori_loop(0, nch, kv_body, 0)
        out = (acc_ref[...] / l_ref[...]).astype(o_ref.dtype)
        for r in range(GROUP):
            h = g * GROUP + r
            o_ref[:, h * HEAD_DIM:(h + 1) * HEAD_DIM] = out[r * QBLK:(r + 1) * QBLK, :]


def _dsa_attention(q, qi, wt, ki, k, v, batch, seq):
    T = batch * seq
    nq = seq // QBLK
    n_heads = q.shape[1]
    kv_dim = k.shape[1]
    k_top = min(TOPK_MAX, seq // 4)
    rows = GROUP * QBLK
    vmem = (2 * (2 * n_heads * QBLK * HEAD_DIM * 2 + seq * IDX_DIM * 2 + 2 * seq * kv_dim * 2
                 + QBLK * n_heads * HEAD_DIM * 2)
            + 2 * seq * LANES * 4 + 3 * rows * LANES * 4
            + IDX_ROWS * IDX_HEADS * QBLK * 4 + 6 * rows * IDX_ROWS * 4)
    hm_spec = pl.BlockSpec((1, n_heads, QBLK, HEAD_DIM), lambda b, i: (b * nq + i, 0, 0, 0))
    return pl.pallas_call(
        functools.partial(_dsa_kernel, k_top=k_top, scale=HEAD_DIM ** -0.5),
        out_shape=jax.ShapeDtypeStruct((T, n_heads * HEAD_DIM), BF16),
        grid=(batch, nq),
        in_specs=[hm_spec, hm_spec,
                  pl.BlockSpec((IDX_HEADS, QBLK), lambda b, i: (0, b * nq + i)),
                  pl.BlockSpec((seq, IDX_DIM), lambda b, i: (b, 0)),
                  pl.BlockSpec((seq, kv_dim), lambda b, i: (b, 0)),
                  pl.BlockSpec((seq, kv_dim), lambda b, i: (b, 0))],
        out_specs=pl.BlockSpec((QBLK, n_heads * HEAD_DIM), lambda b, i: (b * nq + i, 0)),
        scratch_shapes=[pltpu.VMEM((seq, LANES), I32),
                        pltpu.VMEM((seq // QBLK, QBLK, LANES), F32),
                        pltpu.VMEM((rows, 1), F32),
                        pltpu.VMEM((rows, 1), F32),
                        pltpu.VMEM((rows, HEAD_DIM), F32)],
        compiler_params=_params(2, vmem),
        name="dsa_attention",
    )(q, qi, wt, ki, k, v)


def _mm_res_kernel(a_ref, w_ref, b_ref, r_ref, o_ref):
    y = jnp.dot(a_ref[...], w_ref[...], preferred_element_type=F32)
    o_ref[...] = r_ref[...] + (y + b_ref[...])


def _mm_res(a, w, bias, res, tm=1024, tn=512):
    T, K = a.shape
    N = w.shape[1]
    vmem = 2 * (tm * K * 2 + K * tn * 2 + 2 * tm * tn * 4) + 2 * tm * tn * 4
    return pl.pallas_call(
        _mm_res_kernel,
        out_shape=jax.ShapeDtypeStruct((T, N), F32),
        grid=(T // tm, N // tn),
        in_specs=[pl.BlockSpec((tm, K), lambda i, j: (i, 0)),
                  pl.BlockSpec((K, tn), lambda i, j: (0, j)),
                  pl.BlockSpec((1, tn), lambda i, j: (0, j)),
                  pl.BlockSpec((tm, tn), lambda i, j: (i, j))],
        out_specs=pl.BlockSpec((tm, tn), lambda i, j: (i, j)),
        compiler_params=_params(2, vmem),
        name="matmul_residual",
    )(a, w, bias.reshape(1, N), res)


def _mlp_kernel(x_ref, g_ref, w1_ref, w2_ref, o_ref, hn_ref):
    f = pl.program_id(1)

    @pl.when(f == 0)
    def _():
        x = x_ref[...]
        ms = jnp.mean(x * x, axis=-1, keepdims=True)
        hn_ref[...] = (x * lax.rsqrt(ms + RMS_EPS) * g_ref[...]).astype(hn_ref.dtype)
        o_ref[...] = x

    a = jnp.maximum(jnp.dot(hn_ref[...], w1_ref[...], preferred_element_type=F32), 0.0)
    a = (a * a).astype(BF16)
    o_ref[...] += jnp.dot(a, w2_ref[...], preferred_element_type=F32)


def _mlp(x, g, w1, w2, tm=512, tf=1024):
    T, D = x.shape
    F = w1.shape[1]
    vmem = 2 * (2 * tm * D * 4 + 2 * D * tf * 2) + tm * D * 2 + tm * tf * 6 + tm * D * 4
    return pl.pallas_call(
        _mlp_kernel,
        out_shape=jax.ShapeDtypeStruct((T, D), F32),
        grid=(T // tm, F // tf),
        in_specs=[pl.BlockSpec((tm, D), lambda i, f: (i, 0)),
                  pl.BlockSpec((1, D), lambda i, f: (0, 0)),
                  pl.BlockSpec((D, tf), lambda i, f: (0, f)),
                  pl.BlockSpec((tf, D), lambda i, f: (f, 0))],
        out_specs=pl.BlockSpec((tm, D), lambda i, f: (i, 0)),
        scratch_shapes=[pltpu.VMEM((tm, D), BF16)],
        compiler_params=_params(2, vmem),
        name="mlp",
    )(x, g.reshape(1, D), w1, w2)


def _ple_kernel(x_ref, xs_ref, p_ref, wg_ref, wp_ref, o_ref, xb_ref):
    @pl.when(pl.program_id(1) == 0)
    def _():
        xb_ref[...] = x_ref[...].astype(xb_ref.dtype)

    gate = jnp.dot(xb_ref[...], wg_ref[...], preferred_element_type=F32)
    emb = jnp.dot(p_ref[...].astype(BF16), wp_ref[...], preferred_element_type=F32)
    o_ref[...] = xs_ref[...] + jax.nn.sigmoid(gate) * emb


def _ple(x, p, wg, wp, tm=1024, tn=512):
    T, D = x.shape
    P = p.shape[1]
    vmem = 2 * (tm * D * 4 + 2 * tm * tn * 4 + tm * P * 4 + D * tn * 2 + P * tn * 2) + tm * D * 2 + 3 * tm * tn * 4
    return pl.pallas_call(
        _ple_kernel,
        out_shape=jax.ShapeDtypeStruct((T, D), F32),
        grid=(T // tm, D // tn),
        in_specs=[pl.BlockSpec((tm, D), lambda i, j: (i, 0)),
                  pl.BlockSpec((tm, tn), lambda i, j: (i, j)),
                  pl.BlockSpec((tm, P), lambda i, j: (i, 0)),
                  pl.BlockSpec((D, tn), lambda i, j: (0, j)),
                  pl.BlockSpec((P, tn), lambda i, j: (0, j))],
        out_specs=pl.BlockSpec((tm, tn), lambda i, j: (i, j)),
        scratch_shapes=[pltpu.VMEM((tm, D), BF16)],
        compiler_params=_params(2, vmem),
        name="ple",
    )(x, x, p, wg, wp)


def _glu_kernel(a_ref, wa_ref, wg_ref, ba_ref, bg_ref, o_ref):
    a = a_ref[...]
    lin = jnp.dot(a, wa_ref[...], preferred_element_type=F32) + ba_ref[...]
    gate = jnp.dot(a, wg_ref[...], preferred_element_type=F32) + bg_ref[...]
    o_ref[...] = lin * jax.nn.sigmoid(gate)


def _glu(a, w, b, tm=1024, tn=512):
    T, K = a.shape
    N = w.shape[1] // 2
    nj = N // tn
    vmem = 2 * (tm * K * 2 + 2 * K * tn * 2 + tm * tn * 4) + 3 * tm * tn * 4
    return pl.pallas_call(
        _glu_kernel,
        out_shape=jax.ShapeDtypeStruct((T, N), F32),
        grid=(T // tm, nj),
        in_specs=[pl.BlockSpec((tm, K), lambda i, j: (i, 0)),
                  pl.BlockSpec((K, tn), lambda i, j: (0, j)),
                  pl.BlockSpec((K, tn), lambda i, j: (0, j + nj)),
                  pl.BlockSpec((1, tn), lambda i, j: (0, j)),
                  pl.BlockSpec((1, tn), lambda i, j: (0, j + nj))],
        out_specs=pl.BlockSpec((tm, tn), lambda i, j: (i, j)),
        compiler_params=_params(2, vmem),
        name="pw1_glu",
    )(a, w, w, b.reshape(1, 2 * N), b.reshape(1, 2 * N))


CONV_HALO = 32


def _conv_ln_kernel(u_ref, halo_ref, w_ref, b_ref, g_ref, beta_ref, o_ref, ubuf_ref, cbuf_ref):
    tm, D = cbuf_ref.shape
    first = pl.program_id(1) == 0
    ubuf_ref[0:CONV_HALO, :] = jnp.where(first, 0.0, halo_ref[0])
    ubuf_ref[CONV_HALO:, :] = u_ref[0]
    for c in range(D // LANES):
        cols = slice(c * LANES, (c + 1) * LANES)
        acc = jnp.zeros((tm, LANES), F32)
        for tap in range(CONV_W):
            start = CONV_HALO - (CONV_W - 1) + tap
            acc = acc + ubuf_ref[start:start + tm, cols] * w_ref[tap:tap + 1, cols]
        cbuf_ref[:, cols] = acc
    y = cbuf_ref[...] + b_ref[...]
    mu = jnp.mean(y, axis=-1, keepdims=True)
    yc = y - mu
    var = jnp.mean(yc * yc, axis=-1, keepdims=True)
    yn = yc * lax.rsqrt(var + LN_EPS) * g_ref[...] + beta_ref[...]
    o_ref[0] = (yn * jax.nn.sigmoid(yn)).astype(o_ref.dtype)


def _conv_ln(u, w_dw, b_dw, ln_g, ln_b, tm=128):
    B, S, D = u.shape
    per = tm // CONV_HALO
    w_pad = jnp.zeros((CONV_HALO, D), F32).at[:CONV_W].set(w_dw)
    vec = pl.BlockSpec((1, D), lambda b, i: (0, 0))
    vmem = 2 * (tm * D * 4 + CONV_HALO * D * 4 + CONV_HALO * D * 4 + tm * D * 2) + (2 * tm + CONV_HALO) * D * 4 + 4 * tm * D * 4
    return pl.pallas_call(
        _conv_ln_kernel,
        out_shape=jax.ShapeDtypeStruct((B, S, D), BF16),
        grid=(B, S // tm),
        in_specs=[pl.BlockSpec((1, tm, D), lambda b, i: (b, i, 0)),
                  pl.BlockSpec((1, CONV_HALO, D), lambda b, i: (b, jnp.maximum(i * per - 1, 0), 0)),
                  pl.BlockSpec((CONV_HALO, D), lambda b, i: (0, 0)),
                  vec, vec, vec],
        out_specs=pl.BlockSpec((1, tm, D), lambda b, i: (b, i, 0)),
        scratch_shapes=[pltpu.VMEM((tm + CONV_HALO, D), F32), pltpu.VMEM((tm, D), F32)],
        compiler_params=_params(2, vmem),
        name="dwconv_ln_silu",
    )(u, u, w_pad, b_dw.reshape(1, D), ln_g.reshape(1, D), ln_b.reshape(1, D))


def kernel(x, p, positions, norm_mix_g, norm_mlp_g, final_g, a_w_in, a_w_out, a_kidx_g, a_kidx_b,
           b_w_pw1, b_b_pw1, b_w_dw, b_b_dw, b_ln_g, b_ln_b, b_w_pw2, b_b_pw2,
           mlp_w1, mlp_w2, ple_w_proj, ple_w_gate):
    B, S, D = x.shape
    T = B * S
    depth = norm_mix_g.shape[0]
    h = x.reshape(T, D)
    tabs = _rope_tables(positions.reshape(T, 1).astype(I32))
    tab_head, tab_idx = tabs[0:2], tabs[2:5]

    q_dim = a_w_out.shape[1]
    idxq_dim = IDX_HEADS * IDX_DIM
    kv_dim = (a_w_in.shape[2] - q_dim - idxq_dim - IDX_DIM - IDX_HEADS) // 2
    o0 = q_dim
    o1 = o0 + kv_dim
    o2 = o1 + kv_dim
    o3 = o2 + idxq_dim
    o4 = o3 + IDX_DIM

    for i in range(depth):
        j = i // 2
        hn = _rmsnorm(h, norm_mix_g[i], BF16)
        if i % 2 == 0:
            w_in = a_w_in[j].astype(BF16)
            q = _proj(hn, w_in[:, :o0], tab_head, HEAD_DIM, True)
            k = _proj(hn, w_in[:, o0:o1], tab_head, HEAD_DIM, False)
            v = _proj(hn, w_in[:, o1:o2], (), 0, False)
            qi = _proj(hn, w_in[:, o2:o3], tab_idx, IDX_ROPE, True)
            ki, wt = _ki_proj(hn, w_in[:, o3:o4], w_in[:, o4:].T, a_kidx_g[j], a_kidx_b[j], tab_idx)
            attn = _dsa_attention(q, qi, wt, ki, k, v, B, S)
            h = _mm_res(attn, a_w_out[j].astype(BF16), jnp.zeros((D,), F32), h)
        else:
            u = _glu(hn, b_w_pw1[j].astype(BF16), b_b_pw1[j])
            c = _conv_ln(u.reshape(B, S, D), b_w_dw[j], b_b_dw[j], b_ln_g[j], b_ln_b[j])
            h = _mm_res(c.reshape(T, D), b_w_pw2[j].astype(BF16), b_b_pw2[j], h)
        h = _mlp(h, norm_mlp_g[i], mlp_w1[i].astype(BF16), mlp_w2[i].astype(BF16))
        h = _ple(h, p[i].reshape(T, -1), ple_w_gate[i].astype(BF16), ple_w_proj[i].astype(BF16))
    return _rmsnorm(h, final_g, F32).reshape(B, S, D)
```

```python
import functools

import numpy as np
import jax
import jax.numpy as jnp
from jax import lax
from jax.experimental import pallas as pl
from jax.experimental.pallas import tpu as pltpu

F32 = jnp.float32
BF16 = jnp.bfloat16
I32 = jnp.int32
I16 = jnp.int16

CHUNK = 64
QBLK = 128
HEAD_DIM = 128
GROUP = 4
IDX_HEADS = 16
IDX_DIM = 128
IDX_ROPE = 64
TOPK_MAX = 256
ROPE_THETA = 10000.0
CONV_W = 31
RMS_EPS = 1e-6
LN_EPS = 1e-5

LANES = 128
V7X_VMEM_BYTES = 64 * 1024 * 1024
VMEM_CAP = V7X_VMEM_BYTES - 8 * 1024 * 1024

INT_MIN = -2 ** 31
HALF_MIN = -2 ** 15
MASK_NEG = -1e30
LOG2_E = 1.4426950408889634
CHUNK_SHIFT = CHUNK.bit_length() - 1
NT_DIMS = (((1,), (1,)), ((), ()))


def _params(ndims, vmem_estimate):
    limit = min(int(vmem_estimate * 1.25) + (4 << 20), VMEM_CAP)
    return pltpu.CompilerParams(dimension_semantics=("arbitrary",) * ndims,
                                vmem_limit_bytes=limit)


def _rmsnorm_kernel(x_ref, g_ref, o_ref):
    x = x_ref[...]
    ms = jnp.mean(x * x, axis=-1, keepdims=True)
    o_ref[...] = (x * lax.rsqrt(ms + RMS_EPS) * g_ref[...]).astype(o_ref.dtype)


def _rmsnorm(x, g, out_dtype, tm=512):
    T, D = x.shape
    return pl.pallas_call(
        _rmsnorm_kernel,
        out_shape=jax.ShapeDtypeStruct((T, D), out_dtype),
        grid=(T // tm,),
        in_specs=[pl.BlockSpec((tm, D), lambda i: (i, 0)),
                  pl.BlockSpec((1, D), lambda i: (0, 0))],
        out_specs=pl.BlockSpec((tm, D), lambda i: (i, 0)),
        compiler_params=_params(1, 4 * tm * D * 4),
        name="rmsnorm",
    )(x, g.reshape(1, D))


def _rope_tables_kernel(pos_ref, inv_ref, sgn_ref, c128_ref, s128_ref, c64_ref, s64a_ref, s64b_ref):
    pos = pos_ref[...].astype(F32)
    ang = pos * inv_ref[0:1, :]
    c128_ref[...] = jnp.cos(ang)
    s128_ref[...] = jnp.sin(ang) * sgn_ref[0:1, :]
    ang = pos * inv_ref[1:2, :]
    c64_ref[...] = jnp.cos(ang)
    sn = jnp.sin(ang)
    s64a_ref[...] = sn * sgn_ref[1:2, :]
    s64b_ref[...] = sn * sgn_ref[2:3, :]


def _rope_tables(pos, tm=1024):
    T = pos.shape[0]
    half = HEAD_DIM // 2
    inv_h = ROPE_THETA ** (-jnp.arange(half, dtype=F32) * (2.0 / HEAD_DIM))
    half_i = IDX_ROPE // 2
    inv_i = ROPE_THETA ** (-jnp.arange(half_i, dtype=F32) * (2.0 / IDX_ROPE))
    inv = jnp.zeros((8, LANES), F32)
    inv = inv.at[0].set(jnp.concatenate([inv_h, inv_h]))
    inv = inv.at[1, :IDX_ROPE].set(jnp.concatenate([inv_i, inv_i]))
    sgn = np.zeros((8, LANES), np.float32)
    sgn[0, :half] = -1.0
    sgn[0, half:] = 1.0
    sgn[1, half_i:IDX_ROPE] = 1.0
    sgn[2, :half_i] = -1.0
    tab = jax.ShapeDtypeStruct((T, LANES), F32)
    row = pl.BlockSpec((tm, LANES), lambda i: (i, 0))
    cst = pl.BlockSpec((8, LANES), lambda i: (0, 0))
    return pl.pallas_call(
        _rope_tables_kernel,
        out_shape=(tab,) * 5,
        grid=(T // tm,),
        in_specs=[pl.BlockSpec((tm, 1), lambda i: (i, 0)), cst, cst],
        out_specs=(row,) * 5,
        compiler_params=_params(1, 16 * tm * LANES * 4),
        name="rope_tables",
    )(pos, inv, jnp.asarray(sgn))


def _proj_kernel(a_ref, w_ref, *rest, rope, head_major):
    o_ref = rest[-1]
    y = jnp.dot(a_ref[...], w_ref[...].astype(BF16), preferred_element_type=F32)
    tm, tn = y.shape
    tabs = [r[...] for r in rest[:-1]]
    for h in range(tn // HEAD_DIM):
        yh = y[:, h * HEAD_DIM:(h + 1) * HEAD_DIM]
        if rope == HEAD_DIM:
            c, s = tabs
            out = yh * c + pltpu.roll(yh, HEAD_DIM // 2, 1) * s
        else:
            c, sa, sb = tabs
            out = (yh * c + pltpu.roll(yh, IDX_ROPE // 2, 1) * sa
                   + pltpu.roll(yh, HEAD_DIM - IDX_ROPE // 2, 1) * sb)
        out = out.astype(o_ref.dtype)
        if head_major:
            o_ref[:, h] = out.reshape(tm // QBLK, QBLK, HEAD_DIM)
        else:
            o_ref[:, h * HEAD_DIM:(h + 1) * HEAD_DIM] = out


def _proj(a, w_stack, layer, col0, N, tabs, rope, head_major, tm=1024, tn=512):
    T, K = a.shape
    tn = min(tn, N)
    assert col0 % tn == 0 and N % tn == 0
    jb = col0 // tn
    nh = tn // HEAD_DIM
    if head_major:
        out_shape = jax.ShapeDtypeStruct((T // QBLK, N // HEAD_DIM, QBLK, HEAD_DIM), BF16)
        out_spec = pl.BlockSpec((tm // QBLK, nh, QBLK, HEAD_DIM), lambda i, j: (i, j, 0, 0))
    else:
        out_shape = jax.ShapeDtypeStruct((T, N), BF16)
        out_spec = pl.BlockSpec((tm, tn), lambda i, j: (i, j))
    tab_spec = pl.BlockSpec((tm, LANES), lambda i, j: (i, 0))
    vmem = (2 * (tm * K * 2 + K * tn * 4 + tm * tn * 2 + len(tabs) * tm * LANES * 4)
            + K * tn * 2 + 3 * tm * tn * 4)
    return pl.pallas_call(
        functools.partial(_proj_kernel, rope=rope, head_major=head_major),
        out_shape=out_shape,
        grid=(T // tm, N // tn),
        in_specs=[pl.BlockSpec((tm, K), lambda i, j: (i, 0)),
                  pl.BlockSpec((None, K, tn), lambda i, j: (layer, 0, jb + j))] + [tab_spec] * len(tabs),
        out_specs=out_spec,
        compiler_params=_params(2, vmem),
        name=f"proj_rope{rope}_{'hm' if head_major else 'flat'}",
    )(a, w_stack, *tabs)


def _ki_kernel(a_ref, wk_ref, wwt_ref, g_ref, b_ref, c_ref, sa_ref, sb_ref, ki_ref, wt_ref, *, w_scale):
    a = a_ref[...]
    y = jnp.dot(a, wk_ref[...].astype(BF16), preferred_element_type=F32)
    mu = jnp.mean(y, axis=-1, keepdims=True)
    yc = y - mu
    var = jnp.mean(yc * yc, axis=-1, keepdims=True)
    yn = yc * lax.rsqrt(var + LN_EPS) * g_ref[...] + b_ref[...]
    out = (yn * c_ref[...] + pltpu.roll(yn, IDX_ROPE // 2, 1) * sa_ref[...]
           + pltpu.roll(yn, IDX_DIM - IDX_ROPE // 2, 1) * sb_ref[...])
    ki_ref[...] = out.astype(ki_ref.dtype)
    wt = lax.dot_general(wwt_ref[...], a, NT_DIMS, preferred_element_type=F32)
    wt_ref[...] = wt * w_scale


def _ki_proj(a, w_stack, layer, col0, wwt, g, b, tabs, tm=1024):
    T, K = a.shape
    assert col0 % IDX_DIM == 0
    jb = col0 // IDX_DIM
    nw = wwt.shape[0]
    tab_spec = pl.BlockSpec((tm, LANES), lambda i: (i, 0))
    vec_spec = pl.BlockSpec((1, IDX_DIM), lambda i: (0, 0))
    w_scale = IDX_HEADS ** -0.5 * IDX_DIM ** -0.5
    return pl.pallas_call(
        functools.partial(_ki_kernel, w_scale=w_scale),
        out_shape=(jax.ShapeDtypeStruct((T, IDX_DIM), BF16), jax.ShapeDtypeStruct((nw, T), F32)),
        grid=(T // tm,),
        in_specs=[pl.BlockSpec((tm, K), lambda i: (i, 0)),
                  pl.BlockSpec((None, K, IDX_DIM), lambda i: (layer, 0, jb)),
                  pl.BlockSpec((nw, K), lambda i: (0, 0)),
                  vec_spec, vec_spec, tab_spec, tab_spec, tab_spec],
        out_specs=(pl.BlockSpec((tm, IDX_DIM), lambda i: (i, 0)),
                   pl.BlockSpec((nw, tm), lambda i: (0, i))),
        compiler_params=_params(1, 2 * (tm * K * 2 + K * IDX_DIM * 4 + 8 * tm * LANES * 4)),
        name="ki_proj",
    )(a, w_stack, wwt, g.reshape(1, IDX_DIM), b.reshape(1, IDX_DIM), *tabs)


IDX_ROWS = 256


def _sortable_key(x):
    b = lax.bitcast_convert_type(x, I32)
    return b ^ ((b >> 31) & 0x7FFFFFFF)


def _dsa_kernel(q_ref, qi_ref, wt_ref, ki_ref, k_ref, vt_ref, o_ref,
                key_ref, hi_ref, lo_ref, bias_ref, m_ref, l_ref, alpha_ref, acc_ref, s_ref, p_ref,
                *, k_top, scale):
    i = pl.program_id(1)
    seq = ki_ref.shape[0]
    n_heads = q_ref.shape[1]
    R = IDX_ROWS
    nch = (i * QBLK + QBLK + R - 1) // R

    q_glob = i * QBLK + lax.broadcasted_iota(I32, (1, LANES), 1)
    adm_limit = ((q_glob >> CHUNK_SHIFT) + 1) << CHUNK_SHIFT

    qi_all = qi_ref[0].reshape(IDX_HEADS * QBLK, IDX_DIM)

    def idx_body(c, carry):
        r0 = pl.multiple_of(c * R, R)
        kib = ki_ref[pl.ds(r0, R), :]
        d = lax.dot_general(kib, qi_all, NT_DIMS, preferred_element_type=F32)
        acc = jnp.zeros((R, LANES), F32)
        for h in range(IDX_HEADS):
            acc = acc + jnp.maximum(d[:, h * QBLK:(h + 1) * QBLK], 0.0) * wt_ref[h:h + 1, :]
        row = r0 + lax.broadcasted_iota(I32, (R, LANES), 0)
        key = jnp.where(row < adm_limit, _sortable_key(acc), INT_MIN)
        key_ref[pl.ds(r0, R), :] = key
        hi_ref[pl.ds(r0, R), :] = (key >> 16).astype(I16)
        return carry

    lax.fori_loop(0, nch, idx_body, 0)

    nch2 = (nch + 1) // 2

    @pl.when(nch % 2 == 1)
    def _():
        r0 = pl.multiple_of(nch * R, R)
        key_ref[pl.ds(r0, R), :] = jnp.full((R, LANES), INT_MIN, I32)
        hi_ref[pl.ds(r0, R), :] = jnp.full((R, LANES), HALF_MIN, I16)

    def count(pred):
        def body(c, cnt):
            r0 = pl.multiple_of(c * R, R)
            blk = key_ref[pl.ds(r0, R), :]
            row = r0 + lax.broadcasted_iota(I32, (R, LANES), 0)
            return cnt + jnp.sum(pred(blk, row).reshape(R // 8, 8, LANES), axis=0)
        cnt = lax.fori_loop(0, nch, body, jnp.zeros((8, LANES), I32))
        return jnp.sum(cnt, axis=0, keepdims=True)

    def count16(ref, pred):
        def body(c, cnt):
            r0 = pl.multiple_of(c * 2 * R, 2 * R)
            hit = pred(ref[pl.ds(r0, 2 * R), :])
            parts = [hit[t * 16:(t + 1) * 16, :] for t in range(2 * R // 16)]
            while len(parts) > 1:
                parts = [a + b for a, b in zip(parts[0::2], parts[1::2])]
            return cnt + parts[0]
        cnt = lax.fori_loop(0, nch2, body, jnp.zeros((16, LANES), I16))
        return jnp.sum(cnt.astype(I32), axis=0, keepdims=True)

    def search16(ref, target):
        def one_pass(b, lo):
            cand = lo + jnp.left_shift(jnp.int32(1), 15 - b)
            cand16 = cand.astype(I16)
            cnt = count16(ref, lambda v: jnp.where(v >= cand16, jnp.int16(1), jnp.int16(0)))
            return jnp.where(cnt >= target, cand, lo)
        return lax.fori_loop(0, 16, one_pass, jnp.full((1, LANES), HALF_MIN, I32))

    t_hi = search16(hi_ref, k_top)
    t_hi16 = t_hi.astype(I16)
    above = count16(hi_ref, lambda v: jnp.where(v > t_hi16, jnp.int16(1), jnp.int16(0)))

    def lo_body(c, carry):
        r0 = pl.multiple_of(c * R, R)
        key = key_ref[pl.ds(r0, R), :]
        lo_half = (key & 0xFFFF) + HALF_MIN
        lo_ref[pl.ds(r0, R), :] = jnp.where((key >> 16) == t_hi, lo_half, HALF_MIN).astype(I16)
        return carry

    lax.fori_loop(0, 2 * nch2, lo_body, 0)
    t_lo = search16(lo_ref, k_top - above)
    thr = (t_hi << 16) | (t_lo - HALF_MIN)

    c_gt = count(lambda key, row: jnp.where(key > thr, 1, 0))
    c_eq = count(lambda key, row: jnp.where(key == thr, 1, 0))
    need = k_top - c_gt
    n_bits = int(seq).bit_length()

    def tie_rows():
        def tie_pass(b, jlim):
            cand = jlim + jnp.left_shift(jnp.int32(1), n_bits - 1 - b)
            cnt = count(lambda key, row: jnp.where(key == thr, jnp.where(row < cand, 1, 0), 0))
            return jnp.where(cnt <= need, cand, jlim)
        return lax.fori_loop(0, n_bits, tie_pass, jnp.zeros((1, LANES), I32))

    has_excess_ties = jnp.max(jnp.where(c_eq > need, 1, 0)) > 0
    jlim = lax.cond(has_excess_ties, tie_rows, lambda: jnp.full((1, LANES), 2 * seq, I32))

    def bias_body(c, carry):
        r0 = pl.multiple_of(c * R, R)
        key = key_ref[pl.ds(r0, R), :]
        row = r0 + lax.broadcasted_iota(I32, (R, LANES), 0)
        tie_ok = jnp.where(row < jlim, 0.0, MASK_NEG)
        val = jnp.where(key > thr, 0.0, jnp.where(key == thr, tie_ok, MASK_NEG))
        bias_ref[pl.ds(r0, R), :] = jnp.where(row < adm_limit, val, MASK_NEG)
        return carry

    lax.fori_loop(0, nch, bias_body, 0)

    n_kv = n_heads // GROUP
    cols = GROUP * QBLK
    m_ref[...] = jnp.full(m_ref.shape, -3e38, F32)
    l_ref[...] = jnp.zeros(l_ref.shape, F32)
    acc_ref[...] = jnp.zeros(acc_ref.shape, F32)

    scale2 = scale * LOG2_E

    def kv_body(c, carry):
        r0 = pl.multiple_of(c * R, R)
        bias = bias_ref[pl.ds(r0, R), :]
        bias = jnp.concatenate([bias] * GROUP, axis=1)
        for g in range(n_kv):
            qg = q_ref[0, g * GROUP:(g + 1) * GROUP].reshape(cols, HEAD_DIM)
            kblk = k_ref[pl.ds(r0, R), g * HEAD_DIM:(g + 1) * HEAD_DIM]
            s = lax.dot_general(kblk, qg, NT_DIMS, preferred_element_type=F32) * scale2 + bias
            s_ref[g] = s
            m_old = m_ref[g]
            m_new = jnp.maximum(m_old, jnp.max(s, axis=0, keepdims=True))
            alpha_ref[g] = jnp.exp2(m_old - m_new)
            m_ref[g] = m_new
        for g in range(n_kv):
            p = jnp.exp2(s_ref[g] - m_ref[g][0:1])
            l_ref[g] = alpha_ref[g] * l_ref[g] + jnp.sum(p, axis=0, keepdims=True)
            p_ref[g] = p.astype(BF16)
        for g in range(n_kv):
            vt = vt_ref[c, g * HEAD_DIM:(g + 1) * HEAD_DIM, :]
            acc_ref[g] = alpha_ref[g][0:1] * acc_ref[g] + jnp.dot(vt, p_ref[g],
                                                                  preferred_element_type=F32)
        return carry

    lax.fori_loop(0, nch, kv_body, 0)
    for g in range(n_kv):
        out_t = acc_ref[g] / l_ref[g][0:1]
        for r in range(GROUP):
            h = g * GROUP + r
            o_ref[:, h * HEAD_DIM:(h + 1) * HEAD_DIM] = (
                out_t[:, r * QBLK:(r + 1) * QBLK].T.astype(o_ref.dtype))


def _dsa_attention(q, qi, wt, ki, k, vt, batch, seq):
    T = batch * seq
    nq = seq // QBLK
    n_heads = q.shape[1]
    kv_dim = k.shape[1]
    n_kv = n_heads // GROUP
    k_top = min(TOPK_MAX, seq // 4)
    cols = GROUP * QBLK
    vmem = (2 * (2 * n_heads * QBLK * HEAD_DIM * 2 + seq * IDX_DIM * 2 + 2 * seq * kv_dim * 2
                 + QBLK * n_heads * HEAD_DIM * 2)
            + 3 * seq * LANES * 4 + n_kv * (24 + HEAD_DIM) * cols * 4
            + IDX_ROWS * IDX_HEADS * QBLK * 4 + 4 * n_kv * cols * IDX_ROWS * 4)
    hm_spec = pl.BlockSpec((1, n_heads, QBLK, HEAD_DIM), lambda b, i: (b * nq + i, 0, 0, 0))
    return pl.pallas_call(
        functools.partial(_dsa_kernel, k_top=k_top, scale=HEAD_DIM ** -0.5),
        out_shape=jax.ShapeDtypeStruct((T, n_heads * HEAD_DIM), BF16),
        grid=(batch, nq),
        in_specs=[hm_spec, hm_spec,
                  pl.BlockSpec((IDX_HEADS, QBLK), lambda b, i: (0, b * nq + i)),
                  pl.BlockSpec((seq, IDX_DIM), lambda b, i: (b, 0)),
                  pl.BlockSpec((seq, kv_dim), lambda b, i: (b, 0)),
                  pl.BlockSpec((seq // IDX_ROWS, kv_dim, IDX_ROWS), lambda b, i: (b, 0, 0))],
        out_specs=pl.BlockSpec((QBLK, n_heads * HEAD_DIM), lambda b, i: (b * nq + i, 0)),
        scratch_shapes=[pltpu.VMEM((seq, LANES), I32),
                        pltpu.VMEM((seq, LANES), I16),
                        pltpu.VMEM((seq, LANES), I16),
                        pltpu.VMEM((seq, LANES), F32),
                        pltpu.VMEM((n_kv, 8, cols), F32),
                        pltpu.VMEM((n_kv, 8, cols), F32),
                        pltpu.VMEM((n_kv, 8, cols), F32),
                        pltpu.VMEM((n_kv, HEAD_DIM, cols), F32),
                        pltpu.VMEM((n_kv, IDX_ROWS, cols), F32),
                        pltpu.VMEM((n_kv, IDX_ROWS, cols), BF16)],
        compiler_params=_params(2, vmem),
        name="dsa_attention",
    )(q, qi, wt, ki, k, vt)


def _proj_t_kernel(a_ref, wt_ref, o_ref):
    y = lax.dot_general(wt_ref[...], a_ref[...], NT_DIMS, preferred_element_type=F32)
    for c in range(o_ref.shape[0]):
        o_ref[c] = y[:, c * IDX_ROWS:(c + 1) * IDX_ROWS].astype(o_ref.dtype)


def _proj_t(a, wt, tm=1024):
    T, K = a.shape
    N = wt.shape[0]
    per = tm // IDX_ROWS
    return pl.pallas_call(
        _proj_t_kernel,
        out_shape=jax.ShapeDtypeStruct((T // IDX_ROWS, N, IDX_ROWS), BF16),
        grid=(T // tm,),
        in_specs=[pl.BlockSpec((tm, K), lambda i: (i, 0)),
                  pl.BlockSpec((N, K), lambda i: (0, 0))],
        out_specs=pl.BlockSpec((per, N, IDX_ROWS), lambda i: (i, 0, 0)),
        compiler_params=_params(1, 2 * (tm * K * 2 + N * K * 2 + N * tm * 2) + 2 * N * tm * 4),
        name="proj_transposed",
    )(a, wt)


def _mm_res_kernel(a_ref, w_ref, b_ref, r_ref, o_ref):
    y = jnp.dot(a_ref[...], w_ref[...].astype(BF16), preferred_element_type=F32)
    o_ref[...] = r_ref[...] + (y + b_ref[...])


def _mm_res(a, w_stack, layer, bias, res, tm=1024, tn=512):
    T, K = a.shape
    N = w_stack.shape[2]
    vmem = 2 * (tm * K * 2 + K * tn * 4 + 2 * tm * tn * 4) + K * tn * 2 + 2 * tm * tn * 4
    return pl.pallas_call(
        _mm_res_kernel,
        out_shape=jax.ShapeDtypeStruct((T, N), F32),
        grid=(T // tm, N // tn),
        in_specs=[pl.BlockSpec((tm, K), lambda i, j: (i, 0)),
                  pl.BlockSpec((None, K, tn), lambda i, j: (layer, 0, j)),
                  pl.BlockSpec((1, tn), lambda i, j: (0, j)),
                  pl.BlockSpec((tm, tn), lambda i, j: (i, j))],
        out_specs=pl.BlockSpec((tm, tn), lambda i, j: (i, j)),
        compiler_params=_params(2, vmem),
        name="matmul_residual",
    )(a, w_stack, bias.reshape(1, N), res)


def _mlp_kernel(x_ref, g_ref, w1_ref, w2_ref, o_ref, hn_ref):
    f = pl.program_id(1)

    @pl.when(f == 0)
    def _():
        x = x_ref[...]
        ms = jnp.mean(x * x, axis=-1, keepdims=True)
        hn_ref[...] = (x * lax.rsqrt(ms + RMS_EPS) * g_ref[...]).astype(hn_ref.dtype)
        o_ref[...] = x

    a = jnp.maximum(jnp.dot(hn_ref[...], w1_ref[...], preferred_element_type=F32), 0.0)
    a = (a * a).astype(BF16)
    o_ref[...] += jnp.dot(a, w2_ref[...], preferred_element_type=F32)


def _mlp(x, g, w1_stack, w2_stack, layer, tm=512, tf=1024):
    T, D = x.shape
    F = w1_stack.shape[2]
    vmem = 2 * (2 * tm * D * 4 + 2 * D * tf * 2) + tm * D * 2 + tm * tf * 6 + tm * D * 4
    return pl.pallas_call(
        _mlp_kernel,
        out_shape=jax.ShapeDtypeStruct((T, D), F32),
        grid=(T // tm, F // tf),
        in_specs=[pl.BlockSpec((tm, D), lambda i, f: (i, 0)),
                  pl.BlockSpec((1, D), lambda i, f: (0, 0)),
                  pl.BlockSpec((None, D, tf), lambda i, f: (layer, 0, f)),
                  pl.BlockSpec((None, tf, D), lambda i, f: (layer, f, 0))],
        out_specs=pl.BlockSpec((tm, D), lambda i, f: (i, 0)),
        scratch_shapes=[pltpu.VMEM((tm, D), BF16)],
        compiler_params=_params(2, vmem),
        name="mlp",
    )(x, g.reshape(1, D), w1_stack, w2_stack)


def _ple_kernel(x_ref, xs_ref, p_ref, wg_ref, wp_ref, o_ref, xb_ref):
    @pl.when(pl.program_id(1) == 0)
    def _():
        xb_ref[...] = x_ref[...].astype(xb_ref.dtype)

    gate = jnp.dot(xb_ref[...], wg_ref[...].astype(BF16), preferred_element_type=F32)
    emb = jnp.dot(p_ref[...].astype(BF16), wp_ref[...].astype(BF16), preferred_element_type=F32)
    o_ref[...] = xs_ref[...] + jax.nn.sigmoid(gate) * emb


def _ple(x, p_stack, wg_stack, wp_stack, layer, tm=1024, tn=512):
    T, D = x.shape
    P = p_stack.shape[2]
    vmem = (2 * (tm * D * 4 + 2 * tm * tn * 4 + tm * P * 4 + D * tn * 4 + P * tn * 4)
            + tm * D * 2 + D * tn * 2 + 3 * tm * tn * 4)
    return pl.pallas_call(
        _ple_kernel,
        out_shape=jax.ShapeDtypeStruct((T, D), F32),
        grid=(T // tm, D // tn),
        in_specs=[pl.BlockSpec((tm, D), lambda i, j: (i, 0)),
                  pl.BlockSpec((tm, tn), lambda i, j: (i, j)),
                  pl.BlockSpec((None, tm, P), lambda i, j: (layer, i, 0)),
                  pl.BlockSpec((None, D, tn), lambda i, j: (layer, 0, j)),
                  pl.BlockSpec((None, P, tn), lambda i, j: (layer, 0, j))],
        out_specs=pl.BlockSpec((tm, tn), lambda i, j: (i, j)),
        scratch_shapes=[pltpu.VMEM((tm, D), BF16)],
        compiler_params=_params(2, vmem),
        name="ple",
    )(x, x, p_stack, wg_stack, wp_stack)


def _glu_kernel(a_ref, wa_ref, wg_ref, ba_ref, bg_ref, o_ref):
    a = a_ref[...]
    lin = jnp.dot(a, wa_ref[...].astype(BF16), preferred_element_type=F32) + ba_ref[...]
    gate = jnp.dot(a, wg_ref[...].astype(BF16), preferred_element_type=F32) + bg_ref[...]
    o_ref[...] = lin * jax.nn.sigmoid(gate)


def _glu(a, w_stack, layer, b, tm=1024, tn=512):
    T, K = a.shape
    N = w_stack.shape[2] // 2
    nj = N // tn
    vmem = 2 * (tm * K * 2 + 2 * K * tn * 4 + tm * tn * 4) + 2 * K * tn * 2 + 3 * tm * tn * 4
    return pl.pallas_call(
        _glu_kernel,
        out_shape=jax.ShapeDtypeStruct((T, N), F32),
        grid=(T // tm, nj),
        in_specs=[pl.BlockSpec((tm, K), lambda i, j: (i, 0)),
                  pl.BlockSpec((None, K, tn), lambda i, j: (layer, 0, j)),
                  pl.BlockSpec((None, K, tn), lambda i, j: (layer, 0, j + nj)),
                  pl.BlockSpec((1, tn), lambda i, j: (0, j)),
                  pl.BlockSpec((1, tn), lambda i, j: (0, j + nj))],
        out_specs=pl.BlockSpec((tm, tn), lambda i, j: (i, j)),
        compiler_params=_params(2, vmem),
        name="pw1_glu",
    )(a, w_stack, w_stack, b.reshape(1, 2 * N), b.reshape(1, 2 * N))


CONV_HALO = 32


def _conv_ln_kernel(u_ref, halo_ref, w_ref, b_ref, g_ref, beta_ref, o_ref, ubuf_ref, cbuf_ref):
    tm, D = cbuf_ref.shape
    first = pl.program_id(1) == 0
    ubuf_ref[0:CONV_HALO, :] = jnp.where(first, 0.0, halo_ref[0])
    ubuf_ref[CONV_HALO:, :] = u_ref[0]
    for c in range(D // LANES):
        cols = slice(c * LANES, (c + 1) * LANES)
        acc = jnp.zeros((tm, LANES), F32)
        for tap in range(CONV_W):
            start = CONV_HALO - (CONV_W - 1) + tap
            acc = acc + ubuf_ref[start:start + tm, cols] * w_ref[tap:tap + 1, cols]
        cbuf_ref[:, cols] = acc
    y = cbuf_ref[...] + b_ref[...]
    mu = jnp.mean(y, axis=-1, keepdims=True)
    yc = y - mu
    var = jnp.mean(yc * yc, axis=-1, keepdims=True)
    yn = yc * lax.rsqrt(var + LN_EPS) * g_ref[...] + beta_ref[...]
    o_ref[0] = (yn * jax.nn.sigmoid(yn)).astype(o_ref.dtype)


def _conv_ln(u, w_dw, b_dw, ln_g, ln_b, tm=128):
    B, S, D = u.shape
    per = tm // CONV_HALO
    w_pad = jnp.zeros((CONV_HALO, D), F32).at[:CONV_W].set(w_dw)
    vec = pl.BlockSpec((1, D), lambda b, i: (0, 0))
    vmem = 2 * (tm * D * 4 + CONV_HALO * D * 4 + CONV_HALO * D * 4 + tm * D * 2) + (2 * tm + CONV_HALO) * D * 4 + 4 * tm * D * 4
    return pl.pallas_call(
        _conv_ln_kernel,
        out_shape=jax.ShapeDtypeStruct((B, S, D), BF16),
        grid=(B, S // tm),
        in_specs=[pl.BlockSpec((1, tm, D), lambda b, i: (b, i, 0)),
                  pl.BlockSpec((1, CONV_HALO, D), lambda b, i: (b, jnp.maximum(i * per - 1, 0), 0)),
                  pl.BlockSpec((CONV_HALO, D), lambda b, i: (0, 0)),
                  vec, vec, vec],
        out_specs=pl.BlockSpec((1, tm, D), lambda b, i: (b, i, 0)),
        scratch_shapes=[pltpu.VMEM((tm + CONV_HALO, D), F32), pltpu.VMEM((tm, D), F32)],
        compiler_params=_params(2, vmem),
        name="dwconv_ln_silu",
    )(u, u, w_pad, b_dw.reshape(1, D), ln_g.reshape(1, D), ln_b.reshape(1, D))


def kernel(x, p, positions, norm_mix_g, norm_mlp_g, final_g, a_w_in, a_w_out, a_kidx_g, a_kidx_b,
           b_w_pw1, b_b_pw1, b_w_dw, b_b_dw, b_ln_g, b_ln_b, b_w_pw2, b_b_pw2,
           mlp_w1, mlp_w2, ple_w_proj, ple_w_gate):
    B, S, D = x.shape
    T = B * S
    depth = norm_mix_g.shape[0]
    h = x.reshape(T, D)
    tabs = _rope_tables(positions.reshape(T, 1).astype(I32))
    tab_head, tab_idx = tabs[0:2], tabs[2:5]

    q_dim = a_w_out.shape[1]
    idxq_dim = IDX_HEADS * IDX_DIM
    kv_dim = (a_w_in.shape[2] - q_dim - idxq_dim - IDX_DIM - IDX_HEADS) // 2
    o0 = q_dim
    o1 = o0 + kv_dim
    o2 = o1 + kv_dim
    o3 = o2 + idxq_dim
    o4 = o3 + IDX_DIM

    w1_bf = mlp_w1.astype(BF16)
    w2_bf = mlp_w2.astype(BF16)
    p_flat = p.reshape(depth, T, p.shape[-1])
    for i in range(depth):
        j = i // 2
        hn = _rmsnorm(h, norm_mix_g[i], BF16)
        if i % 2 == 0:
            q = _proj(hn, a_w_in, j, 0, o0, tab_head, HEAD_DIM, True)
            k = _proj(hn, a_w_in, j, o0, kv_dim, tab_head, HEAD_DIM, False)
            vt = _proj_t(hn, a_w_in[j, :, o1:o2].T.astype(BF16))
            qi = _proj(hn, a_w_in, j, o2, idxq_dim, tab_idx, IDX_ROPE, True)
            ki, wt = _ki_proj(hn, a_w_in, j, o3, a_w_in[j, :, o4:].T.astype(BF16),
                              a_kidx_g[j], a_kidx_b[j], tab_idx)
            attn = _dsa_attention(q, qi, wt, ki, k, vt, B, S)
            h = _mm_res(attn, a_w_out, j, jnp.zeros((D,), F32), h)
        else:
            u = _glu(hn, b_w_pw1, j, b_b_pw1[j])
            c = _conv_ln(u.reshape(B, S, D), b_w_dw[j], b_b_dw[j], b_ln_g[j], b_ln_b[j])
            h = _mm_res(c.reshape(T, D), b_w_pw2, j, b_b_pw2[j], h)
        h = _mlp(h, norm_mlp_g[i], w1_bf, w2_bf, i)
        h = _ple(h, p_flat, ple_w_gate, ple_w_proj, i)
    return _rmsnorm(h, final_g, F32).reshape(B, S, D)
```

```python
import functools

import numpy as np
import jax
import jax.numpy as jnp
from jax import lax
from jax.experimental import pallas as pl
from jax.experimental.pallas import tpu as pltpu

F32 = jnp.float32
BF16 = jnp.bfloat16
I32 = jnp.int32

CHUNK = 64
QBLK = 128
HEAD_DIM = 128
GROUP = 4
IDX_HEADS = 16
IDX_DIM = 128
IDX_ROPE = 64
TOPK_MAX = 256
ROPE_THETA = 10000.0
CONV_W = 31
RMS_EPS = 1e-6
LN_EPS = 1e-5

LANES = 128
SUBLANES = 8
V7X_VMEM_BYTES = 64 * 1024 * 1024
VMEM_CAP = V7X_VMEM_BYTES - 8 * 1024 * 1024

INT_MIN = -2 ** 31
MASK_NEG = -1e30
LOG2_E = 1.4426950408889634
CHUNK_SHIFT = CHUNK.bit_length() - 1
NT_DIMS = (((1,), (1,)), ((), ()))


def _params(ndims, vmem_estimate):
    limit = min(int(vmem_estimate * 1.25) + (4 << 20), VMEM_CAP)
    return pltpu.CompilerParams(dimension_semantics=("arbitrary",) * ndims,
                                vmem_limit_bytes=limit)


def _rmsnorm_kernel(x_ref, g_ref, o_ref):
    x = x_ref[...]
    ms = jnp.mean(x * x, axis=-1, keepdims=True)
    o_ref[...] = (x * lax.rsqrt(ms + RMS_EPS) * g_ref[...]).astype(o_ref.dtype)


def _rmsnorm(x, g, out_dtype, tm=512):
    T, D = x.shape
    return pl.pallas_call(
        _rmsnorm_kernel,
        out_shape=jax.ShapeDtypeStruct((T, D), out_dtype),
        grid=(T // tm,),
        in_specs=[pl.BlockSpec((tm, D), lambda i: (i, 0)),
                  pl.BlockSpec((1, D), lambda i: (0, 0))],
        out_specs=pl.BlockSpec((tm, D), lambda i: (i, 0)),
        compiler_params=_params(1, 4 * tm * D * 4),
        name="rmsnorm",
    )(x, g.reshape(1, D))


def _rope_tables_kernel(pos_ref, inv_ref, sgn_ref, c128_ref, s128_ref, c64_ref, s64a_ref, s64b_ref):
    pos = pos_ref[...].astype(F32)
    ang = pos * inv_ref[0:1, :]
    c128_ref[...] = jnp.cos(ang)
    s128_ref[...] = jnp.sin(ang) * sgn_ref[0:1, :]
    ang = pos * inv_ref[1:2, :]
    c64_ref[...] = jnp.cos(ang)
    sn = jnp.sin(ang)
    s64a_ref[...] = sn * sgn_ref[1:2, :]
    s64b_ref[...] = sn * sgn_ref[2:3, :]


def _rope_tables(pos, tm=1024):
    T = pos.shape[0]
    half = HEAD_DIM // 2
    inv_h = ROPE_THETA ** (-jnp.arange(half, dtype=F32) * (2.0 / HEAD_DIM))
    half_i = IDX_ROPE // 2
    inv_i = ROPE_THETA ** (-jnp.arange(half_i, dtype=F32) * (2.0 / IDX_ROPE))
    inv = jnp.zeros((8, LANES), F32)
    inv = inv.at[0].set(jnp.concatenate([inv_h, inv_h]))
    inv = inv.at[1, :IDX_ROPE].set(jnp.concatenate([inv_i, inv_i]))
    sgn = np.zeros((8, LANES), np.float32)
    sgn[0, :half] = -1.0
    sgn[0, half:] = 1.0
    sgn[1, half_i:IDX_ROPE] = 1.0
    sgn[2, :half_i] = -1.0
    tab = jax.ShapeDtypeStruct((T, LANES), F32)
    row = pl.BlockSpec((tm, LANES), lambda i: (i, 0))
    cst = pl.BlockSpec((8, LANES), lambda i: (0, 0))
    return pl.pallas_call(
        _rope_tables_kernel,
        out_shape=(tab,) * 5,
        grid=(T // tm,),
        in_specs=[pl.BlockSpec((tm, 1), lambda i: (i, 0)), cst, cst],
        out_specs=(row,) * 5,
        compiler_params=_params(1, 16 * tm * LANES * 4),
        name="rope_tables",
    )(pos, inv, jnp.asarray(sgn))


def _proj_kernel(a_ref, w_ref, *rest, rope, head_major, out_scale):
    o_ref = rest[-1]
    y = jnp.dot(a_ref[...], w_ref[...].astype(BF16), preferred_element_type=F32)
    tm, tn = y.shape
    tabs = [r[...] for r in rest[:-1]]
    for h in range(tn // HEAD_DIM):
        yh = y[:, h * HEAD_DIM:(h + 1) * HEAD_DIM]
        if rope == HEAD_DIM:
            c, s = tabs
            out = yh * c + pltpu.roll(yh, HEAD_DIM // 2, 1) * s
        else:
            c, sa, sb = tabs
            out = (yh * c + pltpu.roll(yh, IDX_ROPE // 2, 1) * sa
                   + pltpu.roll(yh, HEAD_DIM - IDX_ROPE // 2, 1) * sb)
        if out_scale is not None:
            out = out * out_scale
        out = out.astype(o_ref.dtype)
        if head_major:
            o_ref[:, h] = out.reshape(tm // QBLK, QBLK, HEAD_DIM)
        else:
            o_ref[:, h * HEAD_DIM:(h + 1) * HEAD_DIM] = out


def _proj(a, w_stack, layer, col0, N, tabs, rope, head_major, out_scale=None, tm=1024, tn=512):
    T, K = a.shape
    tn = min(tn, N)
    assert col0 % tn == 0 and N % tn == 0
    jb = col0 // tn
    nh = tn // HEAD_DIM
    if head_major:
        out_shape = jax.ShapeDtypeStruct((T // QBLK, N // HEAD_DIM, QBLK, HEAD_DIM), BF16)
        out_spec = pl.BlockSpec((tm // QBLK, nh, QBLK, HEAD_DIM), lambda i, j: (i, j, 0, 0))
    else:
        out_shape = jax.ShapeDtypeStruct((T, N), BF16)
        out_spec = pl.BlockSpec((tm, tn), lambda i, j: (i, j))
    tab_spec = pl.BlockSpec((tm, LANES), lambda i, j: (i, 0))
    vmem = (2 * (tm * K * 2 + K * tn * 4 + tm * tn * 2 + len(tabs) * tm * LANES * 4)
            + K * tn * 2 + 3 * tm * tn * 4)
    return pl.pallas_call(
        functools.partial(_proj_kernel, rope=rope, head_major=head_major, out_scale=out_scale),
        out_shape=out_shape,
        grid=(T // tm, N // tn),
        in_specs=[pl.BlockSpec((tm, K), lambda i, j: (i, 0)),
                  pl.BlockSpec((None, K, tn), lambda i, j: (layer, 0, jb + j))] + [tab_spec] * len(tabs),
        out_specs=out_spec,
        compiler_params=_params(2, vmem),
        name=f"proj_rope{rope}_{'hm' if head_major else 'flat'}",
    )(a, w_stack, *tabs)


def _ki_kernel(a_ref, wk_ref, wwt_ref, g_ref, b_ref, c_ref, sa_ref, sb_ref, ki_ref, wt_ref, *, w_scale):
    a = a_ref[...]
    y = jnp.dot(a, wk_ref[...].astype(BF16), preferred_element_type=F32)
    mu = jnp.mean(y, axis=-1, keepdims=True)
    yc = y - mu
    var = jnp.mean(yc * yc, axis=-1, keepdims=True)
    yn = yc * lax.rsqrt(var + LN_EPS) * g_ref[...] + b_ref[...]
    out = (yn * c_ref[...] + pltpu.roll(yn, IDX_ROPE // 2, 1) * sa_ref[...]
           + pltpu.roll(yn, IDX_DIM - IDX_ROPE // 2, 1) * sb_ref[...])
    ki_ref[...] = out.astype(ki_ref.dtype)
    wt = lax.dot_general(wwt_ref[...], a, NT_DIMS, preferred_element_type=F32)
    wt_ref[...] = wt * w_scale


def _ki_proj(a, w_stack, layer, col0, wwt, g, b, tabs, tm=1024):
    T, K = a.shape
    assert col0 % IDX_DIM == 0
    jb = col0 // IDX_DIM
    nw = wwt.shape[0]
    tab_spec = pl.BlockSpec((tm, LANES), lambda i: (i, 0))
    vec_spec = pl.BlockSpec((1, IDX_DIM), lambda i: (0, 0))
    w_scale = IDX_HEADS ** -0.5 * IDX_DIM ** -0.5
    return pl.pallas_call(
        functools.partial(_ki_kernel, w_scale=w_scale),
        out_shape=(jax.ShapeDtypeStruct((T, IDX_DIM), BF16), jax.ShapeDtypeStruct((nw, T), F32)),
        grid=(T // tm,),
        in_specs=[pl.BlockSpec((tm, K), lambda i: (i, 0)),
                  pl.BlockSpec((None, K, IDX_DIM), lambda i: (layer, 0, jb)),
                  pl.BlockSpec((nw, K), lambda i: (0, 0)),
                  vec_spec, vec_spec, tab_spec, tab_spec, tab_spec],
        out_specs=(pl.BlockSpec((tm, IDX_DIM), lambda i: (i, 0)),
                   pl.BlockSpec((nw, tm), lambda i: (0, i))),
        compiler_params=_params(1, 2 * (tm * K * 2 + K * IDX_DIM * 4 + 8 * tm * LANES * 4)),
        name="ki_proj",
    )(a, w_stack, wwt, g.reshape(1, IDX_DIM), b.reshape(1, IDX_DIM), *tabs)


IDX_ROWS = 256


def _sortable_key(x):
    b = lax.bitcast_convert_type(x, I32)
    return b ^ ((b >> 31) & 0x7FFFFFFF)


def _dsa_kernel(q_ref, qi_ref, wt_ref, ki_ref, k_ref, vt_ref, eye_ref, o_ref,
                key_ref, bias_ref, m_ref, l_ref, alpha_ref, acc_ref, s_ref, p_ref,
                *, k_top):
    i = pl.program_id(1)
    seq = ki_ref.shape[0]
    n_heads = q_ref.shape[1]
    R = IDX_ROWS
    nch = (i * QBLK + QBLK + R - 1) // R

    q_glob = i * QBLK + lax.broadcasted_iota(I32, (1, LANES), 1)
    adm_limit = ((q_glob >> CHUNK_SHIFT) + 1) << CHUNK_SHIFT

    qi_all = qi_ref[0].reshape(IDX_HEADS * QBLK, IDX_DIM)

    def idx_body(c, carry):
        r0 = pl.multiple_of(c * R, R)
        kib = ki_ref[pl.ds(r0, R), :]
        d = lax.dot_general(kib, qi_all, NT_DIMS, preferred_element_type=F32)
        acc = jnp.zeros((R, LANES), F32)
        for h in range(IDX_HEADS):
            acc = acc + jnp.maximum(d[:, h * QBLK:(h + 1) * QBLK], 0.0) * wt_ref[h:h + 1, :]
        row = r0 + lax.broadcasted_iota(I32, (R, LANES), 0)
        key = jnp.where(row < adm_limit, _sortable_key(acc), INT_MIN)
        key_ref[pl.ds(r0, R), :] = key
        return carry

    lax.fori_loop(0, nch, idx_body, 0)

    nch2 = (nch + 1) // 2

    @pl.when(nch % 2 == 1)
    def _():
        r0 = pl.multiple_of(nch * R, R)
        key_ref[pl.ds(r0, R), :] = jnp.full((R, LANES), INT_MIN, I32)

    def count(pred):
        def body(c, cnt):
            r0 = pl.multiple_of(c * R, R)
            blk = key_ref[pl.ds(r0, R), :]
            row = r0 + lax.broadcasted_iota(I32, (R, LANES), 0)
            return cnt + jnp.sum(pred(blk, row).reshape(R // 8, 8, LANES), axis=0)
        cnt = lax.fori_loop(0, nch, body, jnp.zeros((8, LANES), I32))
        return jnp.sum(cnt, axis=0, keepdims=True)

    def count_ge(cand):
        def body(c, cnt):
            r0 = pl.multiple_of(c * 2 * R, 2 * R)
            hit = jnp.where(key_ref[pl.ds(r0, 2 * R), :] >= cand, 1, 0)
            parts = [hit[t * 8:(t + 1) * 8, :] for t in range(2 * R // 8)]
            while len(parts) > 1:
                parts = [a + b for a, b in zip(parts[0::2], parts[1::2])]
            return cnt + parts[0]
        cnt = lax.fori_loop(0, nch2, body, jnp.zeros((8, LANES), I32))
        return jnp.sum(cnt, axis=0, keepdims=True)

    def thr_pass(b, lo):
        cand = lo + jnp.left_shift(jnp.int32(1), 31 - b)
        return jnp.where(count_ge(cand) >= k_top, cand, lo)

    thr = lax.fori_loop(0, 32, thr_pass, jnp.full((1, LANES), INT_MIN, I32))

    c_gt = count(lambda key, row: jnp.where(key > thr, 1, 0))
    c_eq = count(lambda key, row: jnp.where(key == thr, 1, 0))
    need = k_top - c_gt
    n_bits = int(seq).bit_length()

    def tie_rows():
        def tie_pass(b, jlim):
            cand = jlim + jnp.left_shift(jnp.int32(1), n_bits - 1 - b)
            cnt = count(lambda key, row: jnp.where(key == thr, jnp.where(row < cand, 1, 0), 0))
            return jnp.where(cnt <= need, cand, jlim)
        return lax.fori_loop(0, n_bits, tie_pass, jnp.zeros((1, LANES), I32))

    has_excess_ties = jnp.max(jnp.where(c_eq > need, 1, 0)) > 0
    jlim = lax.cond(has_excess_ties, tie_rows, lambda: jnp.full((1, LANES), 2 * seq, I32))

    def bias_body(c, carry):
        r0 = pl.multiple_of(c * R, R)
        key = key_ref[pl.ds(r0, R), :]
        row = r0 + lax.broadcasted_iota(I32, (R, LANES), 0)
        tie_ok = jnp.where(row < jlim, 0.0, MASK_NEG)
        val = jnp.where(key > thr, 0.0, jnp.where(key == thr, tie_ok, MASK_NEG))
        bias_ref[pl.ds(r0, R), :] = jnp.where(row < adm_limit, val, MASK_NEG).astype(BF16)
        return carry

    lax.fori_loop(0, nch, bias_body, 0)

    n_kv = n_heads // GROUP
    cols = GROUP * QBLK
    m_ref[...] = jnp.full(m_ref.shape, -3e38, F32)
    l_ref[...] = jnp.zeros(l_ref.shape, F32)
    acc_ref[...] = jnp.zeros(acc_ref.shape, F32)

    def scores(c, slot):
        r0 = pl.multiple_of(c * R, R)
        mask = bias_ref[pl.ds(r0, R), :]
        for g in range(n_kv):
            qg = q_ref[0, g * GROUP:(g + 1) * GROUP].reshape(cols, HEAD_DIM)
            kblk = k_ref[pl.ds(r0, R), g * HEAD_DIM:(g + 1) * HEAD_DIM]
            s = lax.dot_general(jnp.concatenate([kblk, mask], axis=1),
                                jnp.concatenate([qg, eye_ref[...]], axis=1),
                                NT_DIMS, preferred_element_type=F32)
            s_ref[slot, g] = s
            m_old = m_ref[1 - slot, g]
            m_new = jnp.maximum(m_old, jnp.max(s, axis=0, keepdims=True))
            alpha_ref[slot, g] = jnp.exp2(m_old - m_new)
            m_ref[slot, g] = m_new

    def accumulate(c, slot):
        for g in range(n_kv):
            p = jnp.exp2(s_ref[slot, g] - m_ref[slot, g][0:1])
            l_ref[g] = alpha_ref[slot, g] * l_ref[g] + jnp.sum(p, axis=0, keepdims=True)
            p_ref[g] = p.astype(BF16)
        for g in range(n_kv):
            vt = vt_ref[c, g * HEAD_DIM:(g + 1) * HEAD_DIM, :]
            acc_ref[g] = alpha_ref[slot, g][0:1] * acc_ref[g] + jnp.dot(
                vt, p_ref[g], preferred_element_type=F32)

    scores(0, 0)

    def pair_body(pair, carry):
        c = 2 * pair
        scores(c + 1, 1)
        accumulate(c, 0)
        scores(c + 2, 0)
        accumulate(c + 1, 1)
        return carry

    lax.fori_loop(0, (nch - 1) // 2, pair_body, 0)

    @pl.when(nch % 2 == 0)
    def _():
        scores(nch - 1, 1)
        accumulate(nch - 2, 0)
        accumulate(nch - 1, 1)

    @pl.when(nch % 2 == 1)
    def _():
        accumulate(nch - 1, 0)

    for g in range(n_kv):
        out_t = acc_ref[g] / l_ref[g][0:1]
        for r in range(GROUP):
            h = g * GROUP + r
            o_ref[:, h * HEAD_DIM:(h + 1) * HEAD_DIM] = (
                out_t[:, r * QBLK:(r + 1) * QBLK].T.astype(o_ref.dtype))


def _dsa_attention(q, qi, wt, ki, k, vt, batch, seq):
    T = batch * seq
    nq = seq // QBLK
    n_heads = q.shape[1]
    kv_dim = k.shape[1]
    n_kv = n_heads // GROUP
    k_top = min(TOPK_MAX, seq // 4)
    cols = GROUP * QBLK
    vmem = (2 * (2 * n_heads * QBLK * HEAD_DIM * 2 + seq * IDX_DIM * 2 + 2 * seq * kv_dim * 2
                 + QBLK * n_heads * HEAD_DIM * 2)
            + 2 * seq * LANES * 4 + n_kv * (24 + HEAD_DIM) * cols * 4
            + IDX_ROWS * IDX_HEADS * QBLK * 4 + 4 * n_kv * cols * IDX_ROWS * 4)
    hm_spec = pl.BlockSpec((1, n_heads, QBLK, HEAD_DIM), lambda b, i: (b * nq + i, 0, 0, 0))
    return pl.pallas_call(
        functools.partial(_dsa_kernel, k_top=k_top),
        out_shape=jax.ShapeDtypeStruct((T, n_heads * HEAD_DIM), BF16),
        grid=(batch, nq),
        in_specs=[hm_spec, hm_spec,
                  pl.BlockSpec((IDX_HEADS, QBLK), lambda b, i: (0, b * nq + i)),
                  pl.BlockSpec((seq, IDX_DIM), lambda b, i: (b, 0)),
                  pl.BlockSpec((seq, kv_dim), lambda b, i: (b, 0)),
                  pl.BlockSpec((seq // IDX_ROWS, kv_dim, IDX_ROWS), lambda b, i: (b, 0, 0)),
                  pl.BlockSpec((cols, QBLK), lambda b, i: (0, 0))],
        out_specs=pl.BlockSpec((QBLK, n_heads * HEAD_DIM), lambda b, i: (b * nq + i, 0)),
        scratch_shapes=[pltpu.VMEM((seq, LANES), I32),
                        pltpu.VMEM((seq, LANES), BF16),
                        pltpu.VMEM((2, n_kv, 8, cols), F32),
                        pltpu.VMEM((n_kv, 8, cols), F32),
                        pltpu.VMEM((2, n_kv, 8, cols), F32),
                        pltpu.VMEM((n_kv, HEAD_DIM, cols), F32),
                        pltpu.VMEM((2, n_kv, IDX_ROWS, cols), F32),
                        pltpu.VMEM((n_kv, IDX_ROWS, cols), BF16)],
        compiler_params=_params(2, vmem),
        name="dsa_attention",
    )(q, qi, wt, ki, k, vt, jnp.tile(jnp.eye(QBLK, dtype=BF16), (GROUP, 1)))


def _proj_t_kernel(a_ref, wt_ref, o_ref):
    y = lax.dot_general(wt_ref[...], a_ref[...], NT_DIMS, preferred_element_type=F32)
    for c in range(o_ref.shape[0]):
        o_ref[c] = y[:, c * IDX_ROWS:(c + 1) * IDX_ROWS].astype(o_ref.dtype)


def _proj_t(a, wt, tm=1024):
    T, K = a.shape
    N = wt.shape[0]
    per = tm // IDX_ROWS
    return pl.pallas_call(
        _proj_t_kernel,
        out_shape=jax.ShapeDtypeStruct((T // IDX_ROWS, N, IDX_ROWS), BF16),
        grid=(T // tm,),
        in_specs=[pl.BlockSpec((tm, K), lambda i: (i, 0)),
                  pl.BlockSpec((N, K), lambda i: (0, 0))],
        out_specs=pl.BlockSpec((per, N, IDX_ROWS), lambda i: (i, 0, 0)),
        compiler_params=_params(1, 2 * (tm * K * 2 + N * K * 2 + N * tm * 2) + 2 * N * tm * 4),
        name="proj_transposed",
    )(a, wt)


COL_CHUNK = 512


def _mm_res_kernel(a_ref, w_ref, b_ref, r_ref, o_ref):
    a = a_ref[...]
    for c in range(o_ref.shape[1] // COL_CHUNK):
        cs = slice(c * COL_CHUNK, (c + 1) * COL_CHUNK)
        y = jnp.dot(a, w_ref[:, cs], preferred_element_type=F32)
        o_ref[:, cs] = r_ref[:, cs] + (y + b_ref[:, cs])


def _mm_res(a, w_stack, layer, bias, res, tm=512):
    T, K = a.shape
    N = w_stack.shape[2]
    vmem = 2 * (tm * K * 2 + K * N * 2 + 2 * tm * N * 4) + 3 * tm * COL_CHUNK * 4
    return pl.pallas_call(
        _mm_res_kernel,
        out_shape=jax.ShapeDtypeStruct((T, N), F32),
        grid=(T // tm,),
        in_specs=[pl.BlockSpec((tm, K), lambda i: (i, 0)),
                  pl.BlockSpec((None, K, N), lambda i: (layer, 0, 0)),
                  pl.BlockSpec((1, N), lambda i: (0, 0)),
                  pl.BlockSpec((tm, N), lambda i: (i, 0))],
        out_specs=pl.BlockSpec((tm, N), lambda i: (i, 0)),
        compiler_params=_params(1, vmem),
        name="matmul_residual",
    )(a, w_stack, bias.reshape(1, N), res)


def _mlp_kernel(x_ref, g_ref, w1_ref, w2_ref, o_ref, hn_ref):
    f = pl.program_id(1)

    @pl.when(f == 0)
    def _():
        x = x_ref[...]
        ms = jnp.mean(x * x, axis=-1, keepdims=True)
        hn_ref[...] = (x * lax.rsqrt(ms + RMS_EPS) * g_ref[...]).astype(hn_ref.dtype)
        o_ref[...] = x

    a = jnp.maximum(jnp.dot(hn_ref[...], w1_ref[...], preferred_element_type=F32), 0.0)
    a = (a * a).astype(BF16)
    o_ref[...] += jnp.dot(a, w2_ref[...], preferred_element_type=F32)


def _mlp(x, g, w1_stack, w2_stack, layer, tm=512, tf=1024):
    T, D = x.shape
    F = w1_stack.shape[2]
    vmem = 2 * (2 * tm * D * 4 + 2 * D * tf * 2) + tm * D * 2 + tm * tf * 6 + tm * D * 4
    return pl.pallas_call(
        _mlp_kernel,
        out_shape=jax.ShapeDtypeStruct((T, D), F32),
        grid=(T // tm, F // tf),
        in_specs=[pl.BlockSpec((tm, D), lambda i, f: (i, 0)),
                  pl.BlockSpec((1, D), lambda i, f: (0, 0)),
                  pl.BlockSpec((None, D, tf), lambda i, f: (layer, 0, f)),
                  pl.BlockSpec((None, tf, D), lambda i, f: (layer, f, 0))],
        out_specs=pl.BlockSpec((tm, D), lambda i, f: (i, 0)),
        scratch_shapes=[pltpu.VMEM((tm, D), BF16)],
        compiler_params=_params(2, vmem),
        name="mlp",
    )(x, g.reshape(1, D), w1_stack, w2_stack)


def _ple_kernel(x_ref, p_ref, wg_ref, wp_ref, g_ref, *refs, emit_h):
    if emit_h:
        h_ref, y_ref = refs
    else:
        y_ref, h_ref = refs
    xb = x_ref[...].astype(BF16)
    pb = p_ref[...].astype(BF16)
    for c in range(h_ref.shape[1] // COL_CHUNK):
        cs = slice(c * COL_CHUNK, (c + 1) * COL_CHUNK)
        gate = jnp.dot(xb, wg_ref[:, cs], preferred_element_type=F32)
        emb = jnp.dot(pb, wp_ref[:, cs], preferred_element_type=F32)
        h_ref[:, cs] = x_ref[:, cs] + jax.nn.sigmoid(gate) * emb
    h = h_ref[...]
    ms = jnp.mean(h * h, axis=-1, keepdims=True)
    y_ref[...] = (h * lax.rsqrt(ms + RMS_EPS) * g_ref[...]).astype(y_ref.dtype)


def _ple(x, p_stack, wg_stack, wp_stack, layer, g, emit_h, y_dtype, tm=512):
    T, D = x.shape
    P = p_stack.shape[2]
    row = pl.BlockSpec((tm, D), lambda i: (i, 0))
    y_shape = jax.ShapeDtypeStruct((T, D), y_dtype)
    if emit_h:
        out_shape, out_specs, scratch = (jax.ShapeDtypeStruct((T, D), F32), y_shape), (row, row), []
    else:
        out_shape, out_specs, scratch = y_shape, row, [pltpu.VMEM((tm, D), F32)]
    vmem = (2 * (3 * tm * D * 4 + tm * P * 4 + D * D * 2 + P * D * 2)
            + tm * D * 2 + 3 * tm * COL_CHUNK * 4)
    return pl.pallas_call(
        functools.partial(_ple_kernel, emit_h=emit_h),
        out_shape=out_shape,
        grid=(T // tm,),
        in_specs=[row,
                  pl.BlockSpec((None, tm, P), lambda i: (layer, i, 0)),
                  pl.BlockSpec((None, D, D), lambda i: (layer, 0, 0)),
                  pl.BlockSpec((None, P, D), lambda i: (layer, 0, 0)),
                  pl.BlockSpec((1, D), lambda i: (0, 0))],
        out_specs=out_specs,
        scratch_shapes=scratch,
        compiler_params=_params(1, vmem),
        name="ple_norm",
    )(x, p_stack, wg_stack, wp_stack, g.reshape(1, D))


def _glu_kernel(a_ref, wa_ref, wg_ref, ba_ref, bg_ref, o_ref):
    a = a_ref[...]
    lin = jnp.dot(a, wa_ref[...].astype(BF16), preferred_element_type=F32) + ba_ref[...]
    gate = jnp.dot(a, wg_ref[...].astype(BF16), preferred_element_type=F32) + bg_ref[...]
    o_ref[...] = lin * jax.nn.sigmoid(gate)


def _glu(a, w_stack, layer, b, tm=1024, tn=512):
    T, K = a.shape
    N = w_stack.shape[2] // 2
    nj = N // tn
    vmem = 2 * (tm * K * 2 + 2 * K * tn * 4 + tm * tn * 4) + 2 * K * tn * 2 + 3 * tm * tn * 4
    return pl.pallas_call(
        _glu_kernel,
        out_shape=jax.ShapeDtypeStruct((T, N), F32),
        grid=(T // tm, nj),
        in_specs=[pl.BlockSpec((tm, K), lambda i, j: (i, 0)),
                  pl.BlockSpec((None, K, tn), lambda i, j: (layer, 0, j)),
                  pl.BlockSpec((None, K, tn), lambda i, j: (layer, 0, j + nj)),
                  pl.BlockSpec((1, tn), lambda i, j: (0, j)),
                  pl.BlockSpec((1, tn), lambda i, j: (0, j + nj))],
        out_specs=pl.BlockSpec((tm, tn), lambda i, j: (i, j)),
        compiler_params=_params(2, vmem),
        name="pw1_glu",
    )(a, w_stack, w_stack, b.reshape(1, 2 * N), b.reshape(1, 2 * N))


CONV_HALO = 32


def _conv_ln_kernel(u_ref, halo_ref, w_ref, b_ref, g_ref, beta_ref, o_ref, ubuf_ref, cbuf_ref,
                    sbuf_ref):
    n_chunks, tm, _ = cbuf_ref.shape
    at_start = pl.program_id(1) == 0
    for c in range(n_chunks):
        cols = slice(c * LANES, (c + 1) * LANES)
        ubuf_ref[c, 0:CONV_HALO, :] = jnp.where(at_start, 0.0, halo_ref[0, :, cols])
        ubuf_ref[c, CONV_HALO:, :] = u_ref[0, :, cols]

    first = CONV_HALO - (CONV_W - 1)

    def chunk_body(c, carry):
        acc = jnp.zeros((tm, LANES), F32)
        for phase in range(SUBLANES):
            taps = [k for k in range(CONV_W) if (first + k) % SUBLANES == phase]
            span = max(first + k for k in taps) - phase + tm
            if phase > 0:
                sbuf_ref[0:span, :] = ubuf_ref[c, phase:phase + span, :]
            for k in taps:
                off = first + k - phase
                rows = ubuf_ref[c, off:off + tm, :] if phase == 0 else sbuf_ref[off:off + tm, :]
                acc = acc + rows * w_ref[c, k:k + 1, :]
        cbuf_ref[c] = acc
        return carry

    lax.fori_loop(0, n_chunks, chunk_body, 0)
    y = jnp.concatenate([cbuf_ref[c] for c in range(n_chunks)], axis=1) + b_ref[...]
    mu = jnp.mean(y, axis=-1, keepdims=True)
    yc = y - mu
    var = jnp.mean(yc * yc, axis=-1, keepdims=True)
    yn = yc * lax.rsqrt(var + LN_EPS) * g_ref[...] + beta_ref[...]
    o_ref[0] = (yn * jax.nn.sigmoid(yn)).astype(o_ref.dtype)


def _conv_ln(u, w_dw, b_dw, ln_g, ln_b, tm=128):
    B, S, D = u.shape
    per = tm // CONV_HALO
    n_chunks = D // LANES
    w_pad = jnp.zeros((CONV_HALO, D), F32).at[:CONV_W].set(w_dw)
    w_chunks = w_pad.reshape(CONV_HALO, n_chunks, LANES).transpose(1, 0, 2)
    vec = pl.BlockSpec((1, D), lambda b, i: (0, 0))
    vmem = (2 * (tm * D * 4 + 2 * CONV_HALO * D * 4 + tm * D * 2)
            + (2 * tm + CONV_HALO) * D * 4 + (tm + CONV_HALO) * LANES * 4 + 4 * tm * D * 4)
    return pl.pallas_call(
        _conv_ln_kernel,
        out_shape=jax.ShapeDtypeStruct((B, S, D), BF16),
        grid=(B, S // tm),
        in_specs=[pl.BlockSpec((1, tm, D), lambda b, i: (b, i, 0)),
                  pl.BlockSpec((1, CONV_HALO, D), lambda b, i: (b, jnp.maximum(i * per - 1, 0), 0)),
                  pl.BlockSpec((n_chunks, CONV_HALO, LANES), lambda b, i: (0, 0, 0)),
                  vec, vec, vec],
        out_specs=pl.BlockSpec((1, tm, D), lambda b, i: (b, i, 0)),
        scratch_shapes=[pltpu.VMEM((n_chunks, tm + CONV_HALO, LANES), F32),
                        pltpu.VMEM((n_chunks, tm, LANES), F32),
                        pltpu.VMEM((tm + CONV_HALO, LANES), F32)],
        compiler_params=_params(2, vmem),
        name="dwconv_ln_silu",
    )(u, u, w_chunks, b_dw.reshape(1, D), ln_g.reshape(1, D), ln_b.reshape(1, D))


def kernel(x, p, positions, norm_mix_g, norm_mlp_g, final_g, a_w_in, a_w_out, a_kidx_g, a_kidx_b,
           b_w_pw1, b_b_pw1, b_w_dw, b_b_dw, b_ln_g, b_ln_b, b_w_pw2, b_b_pw2,
           mlp_w1, mlp_w2, ple_w_proj, ple_w_gate):
    B, S, D = x.shape
    T = B * S
    depth = norm_mix_g.shape[0]
    h = x.reshape(T, D)
    tabs = _rope_tables(positions.reshape(T, 1).astype(I32))
    tab_head, tab_idx = tabs[0:2], tabs[2:5]

    q_dim = a_w_out.shape[1]
    idxq_dim = IDX_HEADS * IDX_DIM
    kv_dim = (a_w_in.shape[2] - q_dim - idxq_dim - IDX_DIM - IDX_HEADS) // 2
    o0 = q_dim
    o1 = o0 + kv_dim
    o2 = o1 + kv_dim
    o3 = o2 + idxq_dim
    o4 = o3 + IDX_DIM

    w1_bf = mlp_w1.astype(BF16)
    w2_bf = mlp_w2.astype(BF16)
    wout_bf = a_w_out.astype(BF16)
    wpw2_bf = b_w_pw2.astype(BF16)
    wgate_bf = ple_w_gate.astype(BF16)
    wproj_bf = ple_w_proj.astype(BF16)
    p_flat = p.reshape(depth, T, p.shape[-1])
    hn = _rmsnorm(h, norm_mix_g[0], BF16)
    for i in range(depth):
        j = i // 2
        if i % 2 == 0:
            q = _proj(hn, a_w_in, j, 0, o0, tab_head, HEAD_DIM, True,
                      out_scale=HEAD_DIM ** -0.5 * LOG2_E)
            k = _proj(hn, a_w_in, j, o0, kv_dim, tab_head, HEAD_DIM, False)
            vt = _proj_t(hn, a_w_in[j, :, o1:o2].T.astype(BF16))
            qi = _proj(hn, a_w_in, j, o2, idxq_dim, tab_idx, IDX_ROPE, True)
            ki, wt = _ki_proj(hn, a_w_in, j, o3, a_w_in[j, :, o4:].T.astype(BF16),
                              a_kidx_g[j], a_kidx_b[j], tab_idx)
            attn = _dsa_attention(q, qi, wt, ki, k, vt, B, S)
            h = _mm_res(attn, wout_bf, j, jnp.zeros((D,), F32), h)
        else:
            u = _glu(hn, b_w_pw1, j, b_b_pw1[j])
            c = _conv_ln(u.reshape(B, S, D), b_w_dw[j], b_b_dw[j], b_ln_g[j], b_ln_b[j])
            h = _mm_res(c.reshape(T, D), wpw2_bf, j, b_b_pw2[j], h)
        h = _mlp(h, norm_mlp_g[i], w1_bf, w2_bf, i)
        if i + 1 < depth:
            h, hn = _ple(h, p_flat, wgate_bf, wproj_bf, i, norm_mix_g[i + 1], True, BF16)
        else:
            out = _ple(h, p_flat, wgate_bf, wproj_bf, i, final_g, False, F32)
    return out.reshape(B, S, D)
```

```python
import functools

import numpy as np
import jax
import jax.numpy as jnp
from jax import lax
from jax.experimental import pallas as pl
from jax.experimental.pallas import tpu as pltpu

F32 = jnp.float32
BF16 = jnp.bfloat16
I32 = jnp.int32

CHUNK = 64
QBLK = 128
HEAD_DIM = 128
GROUP = 4
IDX_HEADS = 16
IDX_DIM = 128
IDX_ROPE = 64
TOPK_MAX = 256
ROPE_THETA = 10000.0
CONV_W = 31
RMS_EPS = 1e-6
LN_EPS = 1e-5

LANES = 128
SUBLANES = 8
V7X_VMEM_BYTES = 64 * 1024 * 1024
VMEM_CAP = V7X_VMEM_BYTES - 8 * 1024 * 1024

INT_MIN = -2 ** 31
MASK_NEG = -1e30
LOG2_E = 1.4426950408889634
CHUNK_SHIFT = CHUNK.bit_length() - 1
NT_DIMS = (((1,), (1,)), ((), ()))


def _params(ndims, vmem_estimate):
    limit = min(int(vmem_estimate * 1.25) + (4 << 20), VMEM_CAP)
    return pltpu.CompilerParams(dimension_semantics=("arbitrary",) * ndims,
                                vmem_limit_bytes=limit)


def _rmsnorm_kernel(x_ref, g_ref, o_ref):
    x = x_ref[...]
    ms = jnp.mean(x * x, axis=-1, keepdims=True)
    o_ref[...] = (x * lax.rsqrt(ms + RMS_EPS) * g_ref[...]).astype(o_ref.dtype)


def _rmsnorm(x, g, out_dtype, tm=512):
    T, D = x.shape
    return pl.pallas_call(
        _rmsnorm_kernel,
        out_shape=jax.ShapeDtypeStruct((T, D), out_dtype),
        grid=(T // tm,),
        in_specs=[pl.BlockSpec((tm, D), lambda i: (i, 0)),
                  pl.BlockSpec((1, D), lambda i: (0, 0))],
        out_specs=pl.BlockSpec((tm, D), lambda i: (i, 0)),
        compiler_params=_params(1, 4 * tm * D * 4),
        name="rmsnorm",
    )(x, g.reshape(1, D))


def _rope_tables_kernel(pos_ref, inv_ref, sgn_ref, c128_ref, s128_ref, c64_ref, s64a_ref, s64b_ref):
    pos = pos_ref[...].astype(F32)
    ang = pos * inv_ref[0:1, :]
    c128_ref[...] = jnp.cos(ang)
    s128_ref[...] = jnp.sin(ang) * sgn_ref[0:1, :]
    ang = pos * inv_ref[1:2, :]
    c64_ref[...] = jnp.cos(ang)
    sn = jnp.sin(ang)
    s64a_ref[...] = sn * sgn_ref[1:2, :]
    s64b_ref[...] = sn * sgn_ref[2:3, :]


def _rope_tables(pos, tm=1024):
    T = pos.shape[0]
    half = HEAD_DIM // 2
    inv_h = ROPE_THETA ** (-jnp.arange(half, dtype=F32) * (2.0 / HEAD_DIM))
    half_i = IDX_ROPE // 2
    inv_i = ROPE_THETA ** (-jnp.arange(half_i, dtype=F32) * (2.0 / IDX_ROPE))
    inv = jnp.zeros((8, LANES), F32)
    inv = inv.at[0].set(jnp.concatenate([inv_h, inv_h]))
    inv = inv.at[1, :IDX_ROPE].set(jnp.concatenate([inv_i, inv_i]))
    sgn = np.zeros((8, LANES), np.float32)
    sgn[0, :half] = -1.0
    sgn[0, half:] = 1.0
    sgn[1, half_i:IDX_ROPE] = 1.0
    sgn[2, :half_i] = -1.0
    tab = jax.ShapeDtypeStruct((T, LANES), F32)
    row = pl.BlockSpec((tm, LANES), lambda i: (i, 0))
    cst = pl.BlockSpec((8, LANES), lambda i: (0, 0))
    return pl.pallas_call(
        _rope_tables_kernel,
        out_shape=(tab,) * 5,
        grid=(T // tm,),
        in_specs=[pl.BlockSpec((tm, 1), lambda i: (i, 0)), cst, cst],
        out_specs=(row,) * 5,
        compiler_params=_params(1, 16 * tm * LANES * 4),
        name="rope_tables",
    )(pos, inv, jnp.asarray(sgn))


def _proj_kernel(a_ref, w_ref, *rest, rope, head_major, out_scale):
    o_ref = rest[-1]
    y = jnp.dot(a_ref[...], w_ref[...], preferred_element_type=F32)
    tm, tn = y.shape
    tabs = [r[...] for r in rest[:-1]]
    for h in range(tn // HEAD_DIM):
        yh = y[:, h * HEAD_DIM:(h + 1) * HEAD_DIM]
        if rope == HEAD_DIM:
            c, s = tabs
            out = yh * c + pltpu.roll(yh, HEAD_DIM // 2, 1) * s
        else:
            c, sa, sb = tabs
            out = (yh * c + pltpu.roll(yh, IDX_ROPE // 2, 1) * sa
                   + pltpu.roll(yh, HEAD_DIM - IDX_ROPE // 2, 1) * sb)
        if out_scale is not None:
            out = out * out_scale
        out = out.astype(o_ref.dtype)
        if head_major:
            o_ref[:, h] = out.reshape(tm // QBLK, QBLK, HEAD_DIM)
        else:
            o_ref[:, h * HEAD_DIM:(h + 1) * HEAD_DIM] = out


def _proj(a, w_stack, layer, col0, N, tabs, rope, head_major, out_scale=None, tm=1024, tn=512):
    T, K = a.shape
    tn = min(tn, N)
    assert col0 % tn == 0 and N % tn == 0
    jb = col0 // tn
    nh = tn // HEAD_DIM
    if head_major:
        out_shape = jax.ShapeDtypeStruct((T // QBLK, N // HEAD_DIM, QBLK, HEAD_DIM), BF16)
        out_spec = pl.BlockSpec((tm // QBLK, nh, QBLK, HEAD_DIM), lambda i, j: (i, j, 0, 0))
    else:
        out_shape = jax.ShapeDtypeStruct((T, N), BF16)
        out_spec = pl.BlockSpec((tm, tn), lambda i, j: (i, j))
    tab_spec = pl.BlockSpec((tm, LANES), lambda i, j: (i, 0))
    vmem = (2 * (tm * K * 2 + K * tn * 2 + tm * tn * 2 + len(tabs) * tm * LANES * 4)
            + 3 * tm * tn * 4)
    return pl.pallas_call(
        functools.partial(_proj_kernel, rope=rope, head_major=head_major, out_scale=out_scale),
        out_shape=out_shape,
        grid=(T // tm, N // tn),
        in_specs=[pl.BlockSpec((tm, K), lambda i, j: (i, 0)),
                  pl.BlockSpec((None, K, tn), lambda i, j: (layer, 0, jb + j))] + [tab_spec] * len(tabs),
        out_specs=out_spec,
        compiler_params=_params(2, vmem),
        name=f"proj_rope{rope}_{'hm' if head_major else 'flat'}",
    )(a, w_stack, *tabs)


def _ki_kernel(a_ref, wk_ref, wwt_ref, g_ref, b_ref, c_ref, sa_ref, sb_ref, ki_ref, wt_ref, *, w_scale):
    a = a_ref[...]
    y = jnp.dot(a, wk_ref[...], preferred_element_type=F32)
    mu = jnp.mean(y, axis=-1, keepdims=True)
    yc = y - mu
    var = jnp.mean(yc * yc, axis=-1, keepdims=True)
    yn = yc * lax.rsqrt(var + LN_EPS) * g_ref[...] + b_ref[...]
    out = (yn * c_ref[...] + pltpu.roll(yn, IDX_ROPE // 2, 1) * sa_ref[...]
           + pltpu.roll(yn, IDX_DIM - IDX_ROPE // 2, 1) * sb_ref[...])
    ki_ref[...] = out.astype(ki_ref.dtype)
    wt = lax.dot_general(wwt_ref[...], a, NT_DIMS, preferred_element_type=F32)
    wt_ref[...] = wt * w_scale


def _ki_proj(a, w_stack, layer, col0, wwt, g, b, tabs, tm=1024):
    T, K = a.shape
    assert col0 % IDX_DIM == 0
    jb = col0 // IDX_DIM
    nw = wwt.shape[0]
    tab_spec = pl.BlockSpec((tm, LANES), lambda i: (i, 0))
    vec_spec = pl.BlockSpec((1, IDX_DIM), lambda i: (0, 0))
    w_scale = IDX_HEADS ** -0.5 * IDX_DIM ** -0.5
    return pl.pallas_call(
        functools.partial(_ki_kernel, w_scale=w_scale),
        out_shape=(jax.ShapeDtypeStruct((T, IDX_DIM), BF16), jax.ShapeDtypeStruct((nw, T), F32)),
        grid=(T // tm,),
        in_specs=[pl.BlockSpec((tm, K), lambda i: (i, 0)),
                  pl.BlockSpec((None, K, IDX_DIM), lambda i: (layer, 0, jb)),
                  pl.BlockSpec((nw, K), lambda i: (0, 0)),
                  vec_spec, vec_spec, tab_spec, tab_spec, tab_spec],
        out_specs=(pl.BlockSpec((tm, IDX_DIM), lambda i: (i, 0)),
                   pl.BlockSpec((nw, tm), lambda i: (0, i))),
        compiler_params=_params(1, 2 * (tm * K * 2 + K * IDX_DIM * 4 + 8 * tm * LANES * 4)),
        name="ki_proj",
    )(a, w_stack, wwt, g.reshape(1, IDX_DIM), b.reshape(1, IDX_DIM), *tabs)


IDX_ROWS = 256


def _sortable_key(x):
    b = lax.bitcast_convert_type(x, I32)
    return b ^ ((b >> 31) & 0x7FFFFFFF)


def _dsa_kernel(q_ref, qi_ref, wt_ref, ki_ref, k_ref, vt_ref, eye_ref, o_ref,
                key_ref, bias_ref, m_ref, l_ref, alpha_ref, acc_ref, s_ref, p_ref,
                *, k_top):
    i = pl.program_id(1)
    seq = ki_ref.shape[0]
    n_heads = q_ref.shape[1]
    R = IDX_ROWS
    nch = (i * QBLK + QBLK + R - 1) // R

    q_glob = i * QBLK + lax.broadcasted_iota(I32, (1, LANES), 1)
    adm_limit = ((q_glob >> CHUNK_SHIFT) + 1) << CHUNK_SHIFT

    qi_all = qi_ref[0].reshape(IDX_HEADS * QBLK, IDX_DIM)

    def idx_chunk(c):
        r0 = pl.multiple_of(c * R, R)
        kib = ki_ref[pl.ds(r0, R), :]
        d = lax.dot_general(kib, qi_all, NT_DIMS, preferred_element_type=F32)
        acc = jnp.zeros((R, LANES), F32)
        for h in range(IDX_HEADS):
            acc = acc + jnp.maximum(d[:, h * QBLK:(h + 1) * QBLK], 0.0) * wt_ref[h:h + 1, :]
        row = r0 + lax.broadcasted_iota(I32, (R, LANES), 0)
        key_ref[pl.ds(r0, R), :] = jnp.where(row < adm_limit, _sortable_key(acc), INT_MIN)

    def idx_pair(pair, carry):
        idx_chunk(2 * pair)
        idx_chunk(2 * pair + 1)
        return carry

    lax.fori_loop(0, nch // 2, idx_pair, 0)

    nch2 = (nch + 1) // 2

    @pl.when(nch % 2 == 1)
    def _():
        idx_chunk(nch - 1)
        r0 = pl.multiple_of(nch * R, R)
        key_ref[pl.ds(r0, R), :] = jnp.full((R, LANES), INT_MIN, I32)

    def count(pred):
        def body(c, cnt):
            r0 = pl.multiple_of(c * R, R)
            blk = key_ref[pl.ds(r0, R), :]
            row = r0 + lax.broadcasted_iota(I32, (R, LANES), 0)
            return cnt + jnp.sum(pred(blk, row).reshape(R // 8, 8, LANES), axis=0)
        cnt = lax.fori_loop(0, nch, body, jnp.zeros((8, LANES), I32))
        return jnp.sum(cnt, axis=0, keepdims=True)

    def count_ge(cand):
        def body(c, cnt):
            r0 = pl.multiple_of(c * 2 * R, 2 * R)
            hit = jnp.where(key_ref[pl.ds(r0, 2 * R), :] >= cand, 1, 0)
            parts = [hit[t * 8:(t + 1) * 8, :] for t in range(2 * R // 8)]
            while len(parts) > 1:
                parts = [a + b for a, b in zip(parts[0::2], parts[1::2])]
            return cnt + parts[0]
        cnt = lax.fori_loop(0, nch2, body, jnp.zeros((8, LANES), I32))
        return jnp.sum(cnt, axis=0, keepdims=True)

    def thr_pass(b, lo):
        cand = lo + jnp.left_shift(jnp.int32(1), 31 - b)
        return jnp.where(count_ge(cand) >= k_top, cand, lo)

    thr = lax.fori_loop(0, 32, thr_pass, jnp.full((1, LANES), INT_MIN, I32))

    c_gt = count(lambda key, row: jnp.where(key > thr, 1, 0))
    c_eq = count(lambda key, row: jnp.where(key == thr, 1, 0))
    need = k_top - c_gt
    n_bits = int(seq).bit_length()

    def tie_rows():
        def tie_pass(b, jlim):
            cand = jlim + jnp.left_shift(jnp.int32(1), n_bits - 1 - b)
            cnt = count(lambda key, row: jnp.where(key == thr, jnp.where(row < cand, 1, 0), 0))
            return jnp.where(cnt <= need, cand, jlim)
        return lax.fori_loop(0, n_bits, tie_pass, jnp.zeros((1, LANES), I32))

    has_excess_ties = jnp.max(jnp.where(c_eq > need, 1, 0)) > 0
    jlim = lax.cond(has_excess_ties, tie_rows, lambda: jnp.full((1, LANES), 2 * seq, I32))

    def bias_body(c, carry):
        r0 = pl.multiple_of(c * R, R)
        key = key_ref[pl.ds(r0, R), :]
        row = r0 + lax.broadcasted_iota(I32, (R, LANES), 0)
        tie_ok = jnp.where(row < jlim, 0.0, MASK_NEG)
        val = jnp.where(key > thr, 0.0, jnp.where(key == thr, tie_ok, MASK_NEG))
        bias_ref[pl.ds(r0, R), :] = jnp.where(row < adm_limit, val, MASK_NEG).astype(BF16)
        return carry

    lax.fori_loop(0, nch, bias_body, 0)

    n_kv = n_heads // GROUP
    cols = GROUP * QBLK
    m_ref[...] = jnp.full(m_ref.shape, -3e38, F32)
    l_ref[...] = jnp.zeros(l_ref.shape, F32)
    acc_ref[...] = jnp.zeros(acc_ref.shape, F32)

    def scores(c, slot):
        r0 = pl.multiple_of(c * R, R)
        mask = bias_ref[pl.ds(r0, R), :]
        for g in range(n_kv):
            qg = q_ref[0, g * GROUP:(g + 1) * GROUP].reshape(cols, HEAD_DIM)
            kblk = k_ref[pl.ds(r0, R), g * HEAD_DIM:(g + 1) * HEAD_DIM]
            s = lax.dot_general(jnp.concatenate([kblk, mask], axis=1),
                                jnp.concatenate([qg, eye_ref[...]], axis=1),
                                NT_DIMS, preferred_element_type=F32)
            s_ref[slot, g] = s
            m_old = m_ref[1 - slot, g]
            m_new = jnp.maximum(m_old, jnp.max(s, axis=0, keepdims=True))
            alpha_ref[slot, g] = jnp.exp2(m_old - m_new)
            m_ref[slot, g] = m_new

    def accumulate(c, slot):
        for g in range(n_kv):
            p = jnp.exp2(s_ref[slot, g] - m_ref[slot, g][0:1])
            l_ref[g] = alpha_ref[slot, g] * l_ref[g] + jnp.sum(p, axis=0, keepdims=True)
            p_ref[g] = p.astype(BF16)
        for g in range(n_kv):
            vt = vt_ref[c, g * HEAD_DIM:(g + 1) * HEAD_DIM, :]
            acc_ref[g] = alpha_ref[slot, g][0:1] * acc_ref[g] + jnp.dot(
                vt, p_ref[g], preferred_element_type=F32)

    scores(0, 0)

    def pair_body(pair, carry):
        c = 2 * pair
        scores(c + 1, 1)
        accumulate(c, 0)
        scores(c + 2, 0)
        accumulate(c + 1, 1)
        return carry

    lax.fori_loop(0, (nch - 1) // 2, pair_body, 0)

    @pl.when(nch % 2 == 0)
    def _():
        scores(nch - 1, 1)
        accumulate(nch - 2, 0)
        accumulate(nch - 1, 1)

    @pl.when(nch % 2 == 1)
    def _():
        accumulate(nch - 1, 0)

    for g in range(n_kv):
        out_t = acc_ref[g] / l_ref[g][0:1]
        for r in range(GROUP):
            h = g * GROUP + r
            o_ref[:, h * HEAD_DIM:(h + 1) * HEAD_DIM] = (
                out_t[:, r * QBLK:(r + 1) * QBLK].T.astype(o_ref.dtype))


def _dsa_attention(q, qi, wt, ki, k, vt, batch, seq):
    T = batch * seq
    nq = seq // QBLK
    n_heads = q.shape[1]
    kv_dim = k.shape[1]
    n_kv = n_heads // GROUP
    k_top = min(TOPK_MAX, seq // 4)
    cols = GROUP * QBLK
    vmem = (2 * (2 * n_heads * QBLK * HEAD_DIM * 2 + seq * IDX_DIM * 2 + 2 * seq * kv_dim * 2
                 + QBLK * n_heads * HEAD_DIM * 2)
            + 2 * seq * LANES * 4 + n_kv * (24 + HEAD_DIM) * cols * 4
            + IDX_ROWS * IDX_HEADS * QBLK * 4 + 4 * n_kv * cols * IDX_ROWS * 4)
    hm_spec = pl.BlockSpec((1, n_heads, QBLK, HEAD_DIM), lambda b, i: (b * nq + i, 0, 0, 0))
    return pl.pallas_call(
        functools.partial(_dsa_kernel, k_top=k_top),
        out_shape=jax.ShapeDtypeStruct((T, n_heads * HEAD_DIM), BF16),
        grid=(batch, nq),
        in_specs=[hm_spec, hm_spec,
                  pl.BlockSpec((IDX_HEADS, QBLK), lambda b, i: (0, b * nq + i)),
                  pl.BlockSpec((seq, IDX_DIM), lambda b, i: (b, 0)),
                  pl.BlockSpec((seq, kv_dim), lambda b, i: (b, 0)),
                  pl.BlockSpec((seq // IDX_ROWS, kv_dim, IDX_ROWS), lambda b, i: (b, 0, 0)),
                  pl.BlockSpec((cols, QBLK), lambda b, i: (0, 0))],
        out_specs=pl.BlockSpec((QBLK, n_heads * HEAD_DIM), lambda b, i: (b * nq + i, 0)),
        scratch_shapes=[pltpu.VMEM((seq, LANES), I32),
                        pltpu.VMEM((seq, LANES), BF16),
                        pltpu.VMEM((2, n_kv, 8, cols), F32),
                        pltpu.VMEM((n_kv, 8, cols), F32),
                        pltpu.VMEM((2, n_kv, 8, cols), F32),
                        pltpu.VMEM((n_kv, HEAD_DIM, cols), F32),
                        pltpu.VMEM((2, n_kv, IDX_ROWS, cols), F32),
                        pltpu.VMEM((n_kv, IDX_ROWS, cols), BF16)],
        compiler_params=_params(2, vmem),
        name="dsa_attention",
    )(q, qi, wt, ki, k, vt, jnp.tile(jnp.eye(QBLK, dtype=BF16), (GROUP, 1)))


def _proj_t_kernel(a_ref, wt_ref, o_ref):
    y = lax.dot_general(wt_ref[...], a_ref[...], NT_DIMS, preferred_element_type=F32)
    for c in range(o_ref.shape[0]):
        o_ref[c] = y[:, c * IDX_ROWS:(c + 1) * IDX_ROWS].astype(o_ref.dtype)


def _proj_t(a, wt, tm=1024):
    T, K = a.shape
    N = wt.shape[0]
    per = tm // IDX_ROWS
    return pl.pallas_call(
        _proj_t_kernel,
        out_shape=jax.ShapeDtypeStruct((T // IDX_ROWS, N, IDX_ROWS), BF16),
        grid=(T // tm,),
        in_specs=[pl.BlockSpec((tm, K), lambda i: (i, 0)),
                  pl.BlockSpec((N, K), lambda i: (0, 0))],
        out_specs=pl.BlockSpec((per, N, IDX_ROWS), lambda i: (i, 0, 0)),
        compiler_params=_params(1, 2 * (tm * K * 2 + N * K * 2 + N * tm * 2) + 2 * N * tm * 4),
        name="proj_transposed",
    )(a, wt)


COL_CHUNK = 512


def _mm_res_kernel(a_ref, w_ref, b_ref, r_ref, o_ref):
    a = a_ref[...]
    for c in range(o_ref.shape[1] // COL_CHUNK):
        cs = slice(c * COL_CHUNK, (c + 1) * COL_CHUNK)
        y = jnp.dot(a, w_ref[:, cs], preferred_element_type=F32)
        o_ref[:, cs] = r_ref[:, cs] + (y + b_ref[:, cs])


def _mm_res(a, w_stack, layer, bias, res, tm=512):
    T, K = a.shape
    N = w_stack.shape[2]
    vmem = 2 * (tm * K * 2 + K * N * 2 + 2 * tm * N * 4) + 3 * tm * COL_CHUNK * 4
    return pl.pallas_call(
        _mm_res_kernel,
        out_shape=jax.ShapeDtypeStruct((T, N), F32),
        grid=(T // tm,),
        in_specs=[pl.BlockSpec((tm, K), lambda i: (i, 0)),
                  pl.BlockSpec((None, K, N), lambda i: (layer, 0, 0)),
                  pl.BlockSpec((1, N), lambda i: (0, 0)),
                  pl.BlockSpec((tm, N), lambda i: (i, 0))],
        out_specs=pl.BlockSpec((tm, N), lambda i: (i, 0)),
        compiler_params=_params(1, vmem),
        name="matmul_residual",
    )(a, w_stack, bias.reshape(1, N), res)


def _mlp_kernel(x_ref, g_ref, w1_ref, w2_ref, o_ref, hn_ref):
    f = pl.program_id(1)

    @pl.when(f == 0)
    def _():
        x = x_ref[...]
        ms = jnp.mean(x * x, axis=-1, keepdims=True)
        hn_ref[...] = (x * lax.rsqrt(ms + RMS_EPS) * g_ref[...]).astype(hn_ref.dtype)
        o_ref[...] = x

    a = jnp.maximum(jnp.dot(hn_ref[...], w1_ref[...], preferred_element_type=F32), 0.0)
    a = (a * a).astype(BF16)
    o_ref[...] += jnp.dot(a, w2_ref[...], preferred_element_type=F32)


def _mlp(x, g, w1_stack, w2_stack, layer, tm=512, tf=1024):
    T, D = x.shape
    F = w1_stack.shape[2]
    vmem = 2 * (2 * tm * D * 4 + 2 * D * tf * 2) + tm * D * 2 + tm * tf * 6 + tm * D * 4
    return pl.pallas_call(
        _mlp_kernel,
        out_shape=jax.ShapeDtypeStruct((T, D), F32),
        grid=(T // tm, F // tf),
        in_specs=[pl.BlockSpec((tm, D), lambda i, f: (i, 0)),
                  pl.BlockSpec((1, D), lambda i, f: (0, 0)),
                  pl.BlockSpec((None, D, tf), lambda i, f: (layer, 0, f)),
                  pl.BlockSpec((None, tf, D), lambda i, f: (layer, f, 0))],
        out_specs=pl.BlockSpec((tm, D), lambda i, f: (i, 0)),
        scratch_shapes=[pltpu.VMEM((tm, D), BF16)],
        compiler_params=_params(2, vmem),
        name="mlp",
    )(x, g.reshape(1, D), w1_stack, w2_stack)


def _ple_kernel(x_ref, p_ref, wg_ref, wp_ref, g_ref, *refs, emit_h):
    if emit_h:
        h_ref, y_ref = refs
    else:
        y_ref, h_ref = refs
    xb = x_ref[...].astype(BF16)
    pb = p_ref[...].astype(BF16)
    for c in range(h_ref.shape[1] // COL_CHUNK):
        cs = slice(c * COL_CHUNK, (c + 1) * COL_CHUNK)
        gate = jnp.dot(xb, wg_ref[:, cs], preferred_element_type=F32)
        emb = jnp.dot(pb, wp_ref[:, cs], preferred_element_type=F32)
        h_ref[:, cs] = x_ref[:, cs] + jax.nn.sigmoid(gate) * emb
    h = h_ref[...]
    ms = jnp.mean(h * h, axis=-1, keepdims=True)
    y_ref[...] = (h * lax.rsqrt(ms + RMS_EPS) * g_ref[...]).astype(y_ref.dtype)


def _ple(x, p_stack, wg_stack, wp_stack, layer, g, emit_h, y_dtype, tm=512):
    T, D = x.shape
    P = p_stack.shape[2]
    row = pl.BlockSpec((tm, D), lambda i: (i, 0))
    y_shape = jax.ShapeDtypeStruct((T, D), y_dtype)
    if emit_h:
        out_shape, out_specs, scratch = (jax.ShapeDtypeStruct((T, D), F32), y_shape), (row, row), []
    else:
        out_shape, out_specs, scratch = y_shape, row, [pltpu.VMEM((tm, D), F32)]
    vmem = (2 * (3 * tm * D * 4 + tm * P * 4 + D * D * 2 + P * D * 2)
            + tm * D * 2 + 3 * tm * COL_CHUNK * 4)
    return pl.pallas_call(
        functools.partial(_ple_kernel, emit_h=emit_h),
        out_shape=out_shape,
        grid=(T // tm,),
        in_specs=[row,
                  pl.BlockSpec((None, tm, P), lambda i: (layer, i, 0)),
                  pl.BlockSpec((None, D, D), lambda i: (layer, 0, 0)),
                  pl.BlockSpec((None, P, D), lambda i: (layer, 0, 0)),
                  pl.BlockSpec((1, D), lambda i: (0, 0))],
        out_specs=out_specs,
        scratch_shapes=scratch,
        compiler_params=_params(1, vmem),
        name="ple_norm",
    )(x, p_stack, wg_stack, wp_stack, g.reshape(1, D))


def _glu_kernel(a_ref, wa_ref, wg_ref, ba_ref, bg_ref, o_ref):
    a = a_ref[...]
    lin = jnp.dot(a, wa_ref[...].astype(BF16), preferred_element_type=F32) + ba_ref[...]
    gate = jnp.dot(a, wg_ref[...].astype(BF16), preferred_element_type=F32) + bg_ref[...]
    o_ref[...] = lin * jax.nn.sigmoid(gate)


def _glu(a, w_stack, layer, b, tm=1024, tn=512):
    T, K = a.shape
    N = w_stack.shape[2] // 2
    nj = N // tn
    vmem = 2 * (tm * K * 2 + 2 * K * tn * 4 + tm * tn * 4) + 2 * K * tn * 2 + 3 * tm * tn * 4
    return pl.pallas_call(
        _glu_kernel,
        out_shape=jax.ShapeDtypeStruct((T, N), F32),
        grid=(T // tm, nj),
        in_specs=[pl.BlockSpec((tm, K), lambda i, j: (i, 0)),
                  pl.BlockSpec((None, K, tn), lambda i, j: (layer, 0, j)),
                  pl.BlockSpec((None, K, tn), lambda i, j: (layer, 0, j + nj)),
                  pl.BlockSpec((1, tn), lambda i, j: (0, j)),
                  pl.BlockSpec((1, tn), lambda i, j: (0, j + nj))],
        out_specs=pl.BlockSpec((tm, tn), lambda i, j: (i, j)),
        compiler_params=_params(2, vmem),
        name="pw1_glu",
    )(a, w_stack, w_stack, b.reshape(1, 2 * N), b.reshape(1, 2 * N))


CONV_HALO = 32


def _conv_ln_kernel(u_ref, halo_ref, w_ref, b_ref, g_ref, beta_ref, o_ref, ubuf_ref, cbuf_ref,
                    sbuf_ref):
    n_chunks, tm, _ = cbuf_ref.shape
    at_start = pl.program_id(1) == 0
    for c in range(n_chunks):
        cols = slice(c * LANES, (c + 1) * LANES)
        ubuf_ref[c, 0:CONV_HALO, :] = jnp.where(at_start, 0.0, halo_ref[0, :, cols])
        ubuf_ref[c, CONV_HALO:, :] = u_ref[0, :, cols]

    first = CONV_HALO - (CONV_W - 1)

    def chunk_body(c, carry):
        acc = jnp.zeros((tm, LANES), F32)
        for phase in range(SUBLANES):
            taps = [k for k in range(CONV_W) if (first + k) % SUBLANES == phase]
            span = max(first + k for k in taps) - phase + tm
            if phase > 0:
                sbuf_ref[0:span, :] = ubuf_ref[c, phase:phase + span, :]
            for k in taps:
                off = first + k - phase
                rows = ubuf_ref[c, off:off + tm, :] if phase == 0 else sbuf_ref[off:off + tm, :]
                acc = acc + rows * w_ref[c, k:k + 1, :]
        cbuf_ref[c] = acc
        return carry

    lax.fori_loop(0, n_chunks, chunk_body, 0)
    y = jnp.concatenate([cbuf_ref[c] for c in range(n_chunks)], axis=1) + b_ref[...]
    mu = jnp.mean(y, axis=-1, keepdims=True)
    yc = y - mu
    var = jnp.mean(yc * yc, axis=-1, keepdims=True)
    yn = yc * lax.rsqrt(var + LN_EPS) * g_ref[...] + beta_ref[...]
    o_ref[0] = (yn * jax.nn.sigmoid(yn)).astype(o_ref.dtype)


def _conv_ln(u, w_dw, b_dw, ln_g, ln_b, tm=128):
    B, S, D = u.shape
    per = tm // CONV_HALO
    n_chunks = D // LANES
    w_pad = jnp.zeros((CONV_HALO, D), F32).at[:CONV_W].set(w_dw)
    w_chunks = w_pad.reshape(CONV_HALO, n_chunks, LANES).transpose(1, 0, 2)
    vec = pl.BlockSpec((1, D), lambda b, i: (0, 0))
    vmem = (2 * (tm * D * 4 + 2 * CONV_HALO * D * 4 + tm * D * 2)
            + (2 * tm + CONV_HALO) * D * 4 + (tm + CONV_HALO) * LANES * 4 + 4 * tm * D * 4)
    return pl.pallas_call(
        _conv_ln_kernel,
        out_shape=jax.ShapeDtypeStruct((B, S, D), BF16),
        grid=(B, S // tm),
        in_specs=[pl.BlockSpec((1, tm, D), lambda b, i: (b, i, 0)),
                  pl.BlockSpec((1, CONV_HALO, D), lambda b, i: (b, jnp.maximum(i * per - 1, 0), 0)),
                  pl.BlockSpec((n_chunks, CONV_HALO, LANES), lambda b, i: (0, 0, 0)),
                  vec, vec, vec],
        out_specs=pl.BlockSpec((1, tm, D), lambda b, i: (b, i, 0)),
        scratch_shapes=[pltpu.VMEM((n_chunks, tm + CONV_HALO, LANES), F32),
                        pltpu.VMEM((n_chunks, tm, LANES), F32),
                        pltpu.VMEM((tm + CONV_HALO, LANES), F32)],
        compiler_params=_params(2, vmem),
        name="dwconv_ln_silu",
    )(u, u, w_chunks, b_dw.reshape(1, D), ln_g.reshape(1, D), ln_b.reshape(1, D))


def kernel(x, p, positions, norm_mix_g, norm_mlp_g, final_g, a_w_in, a_w_out, a_kidx_g, a_kidx_b,
           b_w_pw1, b_b_pw1, b_w_dw, b_b_dw, b_ln_g, b_ln_b, b_w_pw2, b_b_pw2,
           mlp_w1, mlp_w2, ple_w_proj, ple_w_gate):
    B, S, D = x.shape
    T = B * S
    depth = norm_mix_g.shape[0]
    h = x.reshape(T, D)
    tabs = _rope_tables(positions.reshape(T, 1).astype(I32))
    tab_head, tab_idx = tabs[0:2], tabs[2:5]

    q_dim = a_w_out.shape[1]
    idxq_dim = IDX_HEADS * IDX_DIM
    kv_dim = (a_w_in.shape[2] - q_dim - idxq_dim - IDX_DIM - IDX_HEADS) // 2
    o0 = q_dim
    o1 = o0 + kv_dim
    o2 = o1 + kv_dim
    o3 = o2 + idxq_dim
    o4 = o3 + IDX_DIM

    w1_bf = mlp_w1.astype(BF16)
    w2_bf = mlp_w2.astype(BF16)
    win_bf = a_w_in.astype(BF16)
    wout_bf = a_w_out.astype(BF16)
    wpw2_bf = b_w_pw2.astype(BF16)
    wgate_bf = ple_w_gate.astype(BF16)
    wproj_bf = ple_w_proj.astype(BF16)
    p_flat = p.reshape(depth, T, p.shape[-1])
    hn = _rmsnorm(h, norm_mix_g[0], BF16)
    for i in range(depth):
        j = i // 2
        if i % 2 == 0:
            q = _proj(hn, win_bf, j, 0, o0, tab_head, HEAD_DIM, True,
                      out_scale=HEAD_DIM ** -0.5 * LOG2_E)
            k = _proj(hn, win_bf, j, o0, kv_dim, tab_head, HEAD_DIM, False)
            vt = _proj_t(hn, win_bf[j, :, o1:o2].T)
            qi = _proj(hn, win_bf, j, o2, idxq_dim, tab_idx, IDX_ROPE, True)
            ki, wt = _ki_proj(hn, win_bf, j, o3, win_bf[j, :, o4:].T,
                              a_kidx_g[j], a_kidx_b[j], tab_idx)
            attn = _dsa_attention(q, qi, wt, ki, k, vt, B, S)
            h = _mm_res(attn, wout_bf, j, jnp.zeros((D,), F32), h)
        else:
            u = _glu(hn, b_w_pw1, j, b_b_pw1[j])
            c = _conv_ln(u.reshape(B, S, D), b_w_dw[j], b_b_dw[j], b_ln_g[j], b_ln_b[j])
            h = _mm_res(c.reshape(T, D), wpw2_bf, j, b_b_pw2[j], h)
        h = _mlp(h, norm_mlp_g[i], w1_bf, w2_bf, i)
        if i + 1 < depth:
            h, hn = _ple(h, p_flat, wgate_bf, wproj_bf, i, norm_mix_g[i + 1], True, BF16)
        else:
            out = _ple(h, p_flat, wgate_bf, wproj_bf, i, final_g, False, F32)
    return out.reshape(B, S, D)
```

```python
import functools

import numpy as np
import jax
import jax.numpy as jnp
from jax import lax
from jax.experimental import pallas as pl
from jax.experimental.pallas import tpu as pltpu

F32 = jnp.float32
BF16 = jnp.bfloat16
I32 = jnp.int32

CHUNK = 64
QBLK = 128
HEAD_DIM = 128
GROUP = 4
IDX_HEADS = 16
IDX_DIM = 128
IDX_ROPE = 64
TOPK_MAX = 256
ROPE_THETA = 10000.0
CONV_W = 31
RMS_EPS = 1e-6
LN_EPS = 1e-5

LANES = 128
SUBLANES = 8
V7X_VMEM_BYTES = 64 * 1024 * 1024
VMEM_CAP = V7X_VMEM_BYTES - 8 * 1024 * 1024

INT_MIN = -2 ** 31
MASK_NEG = -1e30
LOG2_E = 1.4426950408889634
CHUNK_SHIFT = CHUNK.bit_length() - 1
NT_DIMS = (((1,), (1,)), ((), ()))


def _params(ndims, vmem_estimate):
    limit = min(int(vmem_estimate * 1.25) + (4 << 20), VMEM_CAP)
    return pltpu.CompilerParams(dimension_semantics=("arbitrary",) * ndims,
                                vmem_limit_bytes=limit)


def _rope_tables_kernel(pos_ref, inv_ref, sgn_ref, c128_ref, s128_ref, c64_ref, s64a_ref, s64b_ref):
    pos = pos_ref[...].astype(F32)
    ang = pos * inv_ref[0:1, :]
    c128_ref[...] = jnp.cos(ang)
    s128_ref[...] = jnp.sin(ang) * sgn_ref[0:1, :]
    ang = pos * inv_ref[1:2, :]
    c64_ref[...] = jnp.cos(ang)
    sn = jnp.sin(ang)
    s64a_ref[...] = sn * sgn_ref[1:2, :]
    s64b_ref[...] = sn * sgn_ref[2:3, :]


def _rope_tables(pos, tm=1024):
    T = pos.shape[0]
    half = HEAD_DIM // 2
    inv_h = ROPE_THETA ** (-jnp.arange(half, dtype=F32) * (2.0 / HEAD_DIM))
    half_i = IDX_ROPE // 2
    inv_i = ROPE_THETA ** (-jnp.arange(half_i, dtype=F32) * (2.0 / IDX_ROPE))
    inv = jnp.zeros((8, LANES), F32)
    inv = inv.at[0].set(jnp.concatenate([inv_h, inv_h]))
    inv = inv.at[1, :IDX_ROPE].set(jnp.concatenate([inv_i, inv_i]))
    sgn = np.zeros((8, LANES), np.float32)
    sgn[0, :half] = -1.0
    sgn[0, half:] = 1.0
    sgn[1, half_i:IDX_ROPE] = 1.0
    sgn[2, :half_i] = -1.0
    tab = jax.ShapeDtypeStruct((T, LANES), F32)
    row = pl.BlockSpec((tm, LANES), lambda i: (i, 0))
    cst = pl.BlockSpec((8, LANES), lambda i: (0, 0))
    return pl.pallas_call(
        _rope_tables_kernel,
        out_shape=(tab,) * 5,
        grid=(T // tm,),
        in_specs=[pl.BlockSpec((tm, 1), lambda i: (i, 0)), cst, cst],
        out_specs=(row,) * 5,
        compiler_params=_params(1, 16 * tm * LANES * 4),
        name="rope_tables",
    )(pos, inv, jnp.asarray(sgn))


PROJ_CHUNK = 512
IDX_ROWS = 256


def _rope_full(yh, c, s):
    return yh * c + pltpu.roll(yh, HEAD_DIM // 2, 1) * s


def _rope_partial(yh, c, sa, sb):
    return (yh * c + pltpu.roll(yh, IDX_ROPE // 2, 1) * sa
            + pltpu.roll(yh, IDX_DIM - IDX_ROPE // 2, 1) * sb)


def _in_proj_kernel(x_ref, g_ref, w_ref, c128_ref, s128_ref, c64_ref, s64a_ref, s64b_ref,
                    lng_ref, lnb_ref, q_ref, k_ref, vt_ref, qi_ref, ki_ref, wt_ref, hn_ref,
                    *, n_q, n_qi, q_scale, w_scale):
    j = pl.program_id(1)

    @pl.when(j == 0)
    def _():
        x = x_ref[...]
        ms = jnp.mean(x * x, axis=-1, keepdims=True)
        hn_ref[...] = (x * lax.rsqrt(ms + RMS_EPS) * g_ref[...]).astype(hn_ref.dtype)

    y = jnp.dot(hn_ref[...], w_ref[...], preferred_element_type=F32)
    tm = y.shape[0]
    heads = [y[:, h * HEAD_DIM:(h + 1) * HEAD_DIM] for h in range(PROJ_CHUNK // HEAD_DIM)]
    j_k, j_v, j_qi, j_ki = n_q, n_q + 1, n_q + 2, n_q + 2 + n_qi

    @pl.when(j < j_k)
    def _():
        for h, yh in enumerate(heads):
            out = _rope_full(yh, c128_ref[...], s128_ref[...]) * q_scale
            q_ref[:, h] = out.astype(q_ref.dtype).reshape(tm // QBLK, QBLK, HEAD_DIM)

    @pl.when(j == j_k)
    def _():
        for h, yh in enumerate(heads):
            out = _rope_full(yh, c128_ref[...], s128_ref[...])
            k_ref[:, h * HEAD_DIM:(h + 1) * HEAD_DIM] = out.astype(k_ref.dtype)

    @pl.when(j == j_v)
    def _():
        yt = y.T
        for c in range(vt_ref.shape[0]):
            vt_ref[c] = yt[:, c * IDX_ROWS:(c + 1) * IDX_ROWS].astype(vt_ref.dtype)

    @pl.when((j >= j_qi) & (j < j_ki))
    def _():
        for h, yh in enumerate(heads):
            out = _rope_partial(yh, c64_ref[...], s64a_ref[...], s64b_ref[...])
            qi_ref[:, h] = out.astype(qi_ref.dtype).reshape(tm // QBLK, QBLK, HEAD_DIM)

    @pl.when(j == j_ki)
    def _():
        yk = heads[0]
        mu = jnp.mean(yk, axis=-1, keepdims=True)
        yc = yk - mu
        var = jnp.mean(yc * yc, axis=-1, keepdims=True)
        yn = yc * lax.rsqrt(var + LN_EPS) * lng_ref[...] + lnb_ref[...]
        ki_ref[...] = _rope_partial(yn, c64_ref[...], s64a_ref[...], s64b_ref[...]).astype(ki_ref.dtype)
        wt_ref[...] = heads[1].T[0:IDX_HEADS, :] * w_scale


def _in_proj(x, g, w_stack, layer, tabs, ln_g, ln_b, q_dim, kv_dim, idxq_dim, tm=1024):
    T, D = x.shape
    assert q_dim % PROJ_CHUNK == 0 and idxq_dim % PROJ_CHUNK == 0 and kv_dim == PROJ_CHUNK
    assert IDX_DIM == HEAD_DIM and IDX_DIM + IDX_HEADS <= 2 * HEAD_DIM
    n_q, n_qi = q_dim // PROJ_CHUNK, idxq_dim // PROJ_CHUNK
    n_steps = n_q + 2 + n_qi + 1
    assert w_stack.shape[2] == n_steps * PROJ_CHUNK
    hpc = PROJ_CHUNK // HEAD_DIM
    tab_spec = pl.BlockSpec((tm, LANES), lambda i, j: (i, 0))
    vec_spec = pl.BlockSpec((1, IDX_DIM), lambda i, j: (0, 0))
    hm_block = (tm // QBLK, hpc, QBLK, HEAD_DIM)
    out_shape = (
        jax.ShapeDtypeStruct((T // QBLK, q_dim // HEAD_DIM, QBLK, HEAD_DIM), BF16),
        jax.ShapeDtypeStruct((T, kv_dim), BF16),
        jax.ShapeDtypeStruct((T // IDX_ROWS, kv_dim, IDX_ROWS), BF16),
        jax.ShapeDtypeStruct((T // QBLK, idxq_dim // HEAD_DIM, QBLK, HEAD_DIM), BF16),
        jax.ShapeDtypeStruct((T, IDX_DIM), BF16),
        jax.ShapeDtypeStruct((IDX_HEADS, T), F32))
    out_specs = (
        pl.BlockSpec(hm_block, lambda i, j: (i, jnp.minimum(j, n_q - 1), 0, 0)),
        pl.BlockSpec((tm, kv_dim), lambda i, j: (i, 0)),
        pl.BlockSpec((tm // IDX_ROWS, kv_dim, IDX_ROWS), lambda i, j: (i, 0, 0)),
        pl.BlockSpec(hm_block, lambda i, j: (i, jnp.clip(j - (n_q + 2), 0, n_qi - 1), 0, 0)),
        pl.BlockSpec((tm, IDX_DIM), lambda i, j: (i, 0)),
        pl.BlockSpec((IDX_HEADS, tm), lambda i, j: (0, i)))
    vmem = (2 * (tm * D * 4 + D * PROJ_CHUNK * 2 + 5 * tm * LANES * 4 + 4 * tm * PROJ_CHUNK * 2)
            + tm * D * 2 + 6 * tm * PROJ_CHUNK * 4)
    return pl.pallas_call(
        functools.partial(_in_proj_kernel, n_q=n_q, n_qi=n_qi,
                          q_scale=HEAD_DIM ** -0.5 * LOG2_E,
                          w_scale=IDX_HEADS ** -0.5 * IDX_DIM ** -0.5),
        out_shape=out_shape,
        grid=(T // tm, n_steps),
        in_specs=[pl.BlockSpec((tm, D), lambda i, j: (i, 0)),
                  pl.BlockSpec((1, D), lambda i, j: (0, 0)),
                  pl.BlockSpec((None, D, PROJ_CHUNK), lambda i, j: (layer, 0, j)),
                  tab_spec, tab_spec, tab_spec, tab_spec, tab_spec, vec_spec, vec_spec],
        out_specs=out_specs,
        scratch_shapes=[pltpu.VMEM((tm, D), BF16)],
        compiler_params=_params(2, vmem),
        name="dsa_in_proj",
    )(x, g.reshape(1, D), w_stack, *tabs, ln_g.reshape(1, IDX_DIM), ln_b.reshape(1, IDX_DIM))


def _sortable_key(x):
    b = lax.bitcast_convert_type(x, I32)
    return b ^ ((b >> 31) & 0x7FFFFFFF)


def _dsa_kernel(q_ref, qi_ref, wt_ref, ki_ref, k_ref, vt_ref, eye_ref, o_ref,
                key_ref, bias_ref, m_ref, l_ref, alpha_ref, acc_ref, s_ref, p_ref,
                *, k_top):
    i = pl.program_id(1)
    seq = ki_ref.shape[0]
    n_heads = q_ref.shape[1]
    R = IDX_ROWS
    nch = (i * QBLK + QBLK + R - 1) // R

    q_glob = i * QBLK + lax.broadcasted_iota(I32, (1, LANES), 1)
    adm_limit = ((q_glob >> CHUNK_SHIFT) + 1) << CHUNK_SHIFT

    qi_all = qi_ref[0].reshape(IDX_HEADS * QBLK, IDX_DIM)

    def idx_chunk(c):
        r0 = pl.multiple_of(c * R, R)
        kib = ki_ref[pl.ds(r0, R), :]
        d = lax.dot_general(kib, qi_all, NT_DIMS, preferred_element_type=F32)
        acc = jnp.zeros((R, LANES), F32)
        for h in range(IDX_HEADS):
            acc = acc + jnp.maximum(d[:, h * QBLK:(h + 1) * QBLK], 0.0) * wt_ref[h:h + 1, :]
        row = r0 + lax.broadcasted_iota(I32, (R, LANES), 0)
        key_ref[pl.ds(r0, R), :] = jnp.where(row < adm_limit, _sortable_key(acc), INT_MIN)

    def idx_pair(pair, carry):
        idx_chunk(2 * pair)
        idx_chunk(2 * pair + 1)
        return carry

    lax.fori_loop(0, nch // 2, idx_pair, 0)

    nch2 = (nch + 1) // 2

    @pl.when(nch % 2 == 1)
    def _():
        idx_chunk(nch - 1)
        r0 = pl.multiple_of(nch * R, R)
        key_ref[pl.ds(r0, R), :] = jnp.full((R, LANES), INT_MIN, I32)

    def count(pred):
        def body(c, cnt):
            r0 = pl.multiple_of(c * R, R)
            blk = key_ref[pl.ds(r0, R), :]
            row = r0 + lax.broadcasted_iota(I32, (R, LANES), 0)
            return cnt + jnp.sum(pred(blk, row).reshape(R // 8, 8, LANES), axis=0)
        cnt = lax.fori_loop(0, nch, body, jnp.zeros((8, LANES), I32))
        return jnp.sum(cnt, axis=0, keepdims=True)

    def count_ge(cand):
        def body(c, cnt):
            r0 = pl.multiple_of(c * 2 * R, 2 * R)
            hit = jnp.where(key_ref[pl.ds(r0, 2 * R), :] >= cand, 1, 0)
            parts = [hit[t * 8:(t + 1) * 8, :] for t in range(2 * R // 8)]
            while len(parts) > 1:
                parts = [a + b for a, b in zip(parts[0::2], parts[1::2])]
            return cnt + parts[0]
        cnt = lax.fori_loop(0, nch2, body, jnp.zeros((8, LANES), I32))
        return jnp.sum(cnt, axis=0, keepdims=True)

    def thr_pass(b, lo):
        cand = lo + jnp.left_shift(jnp.int32(1), 31 - b)
        return jnp.where(count_ge(cand) >= k_top, cand, lo)

    thr = lax.fori_loop(0, 32, thr_pass, jnp.full((1, LANES), INT_MIN, I32))

    c_gt = count(lambda key, row: jnp.where(key > thr, 1, 0))
    c_eq = count(lambda key, row: jnp.where(key == thr, 1, 0))
    need = k_top - c_gt
    n_bits = int(seq).bit_length()

    def tie_rows():
        def tie_pass(b, jlim):
            cand = jlim + jnp.left_shift(jnp.int32(1), n_bits - 1 - b)
            cnt = count(lambda key, row: jnp.where(key == thr, jnp.where(row < cand, 1, 0), 0))
            return jnp.where(cnt <= need, cand, jlim)
        return lax.fori_loop(0, n_bits, tie_pass, jnp.zeros((1, LANES), I32))

    has_excess_ties = jnp.max(jnp.where(c_eq > need, 1, 0)) > 0
    jlim = lax.cond(has_excess_ties, tie_rows, lambda: jnp.full((1, LANES), 2 * seq, I32))

    def bias_body(c, carry):
        r0 = pl.multiple_of(c * R, R)
        key = key_ref[pl.ds(r0, R), :]
        row = r0 + lax.broadcasted_iota(I32, (R, LANES), 0)
        tie_ok = jnp.where(row < jlim, 0.0, MASK_NEG)
        val = jnp.where(key > thr, 0.0, jnp.where(key == thr, tie_ok, MASK_NEG))
        bias_ref[pl.ds(r0, R), :] = jnp.where(row < adm_limit, val, MASK_NEG).astype(BF16)
        return carry

    lax.fori_loop(0, nch, bias_body, 0)

    n_kv = n_heads // GROUP
    cols = GROUP * QBLK
    m_ref[...] = jnp.full(m_ref.shape, -3e38, F32)
    l_ref[...] = jnp.zeros(l_ref.shape, F32)
    acc_ref[...] = jnp.zeros(acc_ref.shape, F32)

    def scores(c, slot):
        r0 = pl.multiple_of(c * R, R)
        mask = bias_ref[pl.ds(r0, R), :]
        for g in range(n_kv):
            qg = q_ref[0, g * GROUP:(g + 1) * GROUP].reshape(cols, HEAD_DIM)
            kblk = k_ref[pl.ds(r0, R), g * HEAD_DIM:(g + 1) * HEAD_DIM]
            s = lax.dot_general(jnp.concatenate([kblk, mask], axis=1),
                                jnp.concatenate([qg, eye_ref[...]], axis=1),
                                NT_DIMS, preferred_element_type=F32)
            s_ref[slot, g] = s
            m_old = m_ref[1 - slot, g]
            m_new = jnp.maximum(m_old, jnp.max(s, axis=0, keepdims=True))
            alpha_ref[slot, g] = jnp.exp2(m_old - m_new)
            m_ref[slot, g] = m_new

    def accumulate(c, slot):
        for g in range(n_kv):
            p = jnp.exp2(s_ref[slot, g] - m_ref[slot, g][0:1])
            l_ref[g] = alpha_ref[slot, g] * l_ref[g] + jnp.sum(p, axis=0, keepdims=True)
            p_ref[g] = p.astype(BF16)
        for g in range(n_kv):
            vt = vt_ref[c, g * HEAD_DIM:(g + 1) * HEAD_DIM, :]
            acc_ref[g] = alpha_ref[slot, g][0:1] * acc_ref[g] + jnp.dot(
                vt, p_ref[g], preferred_element_type=F32)

    scores(0, 0)

    def pair_body(pair, carry):
        c = 2 * pair
        scores(c + 1, 1)
        accumulate(c, 0)
        scores(c + 2, 0)
        accumulate(c + 1, 1)
        return carry

    lax.fori_loop(0, (nch - 1) // 2, pair_body, 0)

    @pl.when(nch % 2 == 0)
    def _():
        scores(nch - 1, 1)
        accumulate(nch - 2, 0)
        accumulate(nch - 1, 1)

    @pl.when(nch % 2 == 1)
    def _():
        accumulate(nch - 1, 0)

    for g in range(n_kv):
        out_t = acc_ref[g] / l_ref[g][0:1]
        for r in range(GROUP):
            h = g * GROUP + r
            o_ref[:, h * HEAD_DIM:(h + 1) * HEAD_DIM] = (
                out_t[:, r * QBLK:(r + 1) * QBLK].T.astype(o_ref.dtype))


def _dsa_attention(q, qi, wt, ki, k, vt, batch, seq):
    T = batch * seq
    nq = seq // QBLK
    n_heads = q.shape[1]
    kv_dim = k.shape[1]
    n_kv = n_heads // GROUP
    k_top = min(TOPK_MAX, seq // 4)
    cols = GROUP * QBLK
    vmem = (2 * (2 * n_heads * QBLK * HEAD_DIM * 2 + seq * IDX_DIM * 2 + 2 * seq * kv_dim * 2
                 + QBLK * n_heads * HEAD_DIM * 2)
            + 2 * seq * LANES * 4 + n_kv * (24 + HEAD_DIM) * cols * 4
            + IDX_ROWS * IDX_HEADS * QBLK * 4 + 4 * n_kv * cols * IDX_ROWS * 4)
    hm_spec = pl.BlockSpec((1, n_heads, QBLK, HEAD_DIM), lambda b, i: (b * nq + i, 0, 0, 0))
    return pl.pallas_call(
        functools.partial(_dsa_kernel, k_top=k_top),
        out_shape=jax.ShapeDtypeStruct((T, n_heads * HEAD_DIM), BF16),
        grid=(batch, nq),
        in_specs=[hm_spec, hm_spec,
                  pl.BlockSpec((IDX_HEADS, QBLK), lambda b, i: (0, b * nq + i)),
                  pl.BlockSpec((seq, IDX_DIM), lambda b, i: (b, 0)),
                  pl.BlockSpec((seq, kv_dim), lambda b, i: (b, 0)),
                  pl.BlockSpec((seq // IDX_ROWS, kv_dim, IDX_ROWS), lambda b, i: (b, 0, 0)),
                  pl.BlockSpec((cols, QBLK), lambda b, i: (0, 0))],
        out_specs=pl.BlockSpec((QBLK, n_heads * HEAD_DIM), lambda b, i: (b * nq + i, 0)),
        scratch_shapes=[pltpu.VMEM((seq, LANES), I32),
                        pltpu.VMEM((seq, LANES), BF16),
                        pltpu.VMEM((2, n_kv, 8, cols), F32),
                        pltpu.VMEM((n_kv, 8, cols), F32),
                        pltpu.VMEM((2, n_kv, 8, cols), F32),
                        pltpu.VMEM((n_kv, HEAD_DIM, cols), F32),
                        pltpu.VMEM((2, n_kv, IDX_ROWS, cols), F32),
                        pltpu.VMEM((n_kv, IDX_ROWS, cols), BF16)],
        compiler_params=_params(2, vmem),
        name="dsa_attention",
    )(q, qi, wt, ki, k, vt, jnp.tile(jnp.eye(QBLK, dtype=BF16), (GROUP, 1)))


COL_CHUNK = 512


def _mm_res_kernel(a_ref, w_ref, b_ref, r_ref, o_ref):
    a = a_ref[...]
    for c in range(o_ref.shape[1] // COL_CHUNK):
        cs = slice(c * COL_CHUNK, (c + 1) * COL_CHUNK)
        y = jnp.dot(a, w_ref[:, cs], preferred_element_type=F32)
        o_ref[:, cs] = r_ref[:, cs] + (y + b_ref[:, cs])


def _mm_res(a, w_stack, layer, bias, res, tm=512):
    T, K = a.shape
    N = w_stack.shape[2]
    vmem = 2 * (tm * K * 2 + K * N * 2 + 2 * tm * N * 4) + 3 * tm * COL_CHUNK * 4
    return pl.pallas_call(
        _mm_res_kernel,
        out_shape=jax.ShapeDtypeStruct((T, N), F32),
        grid=(T // tm,),
        in_specs=[pl.BlockSpec((tm, K), lambda i: (i, 0)),
                  pl.BlockSpec((None, K, N), lambda i: (layer, 0, 0)),
                  pl.BlockSpec((1, N), lambda i: (0, 0)),
                  pl.BlockSpec((tm, N), lambda i: (i, 0))],
        out_specs=pl.BlockSpec((tm, N), lambda i: (i, 0)),
        compiler_params=_params(1, vmem),
        name="matmul_residual",
    )(a, w_stack, bias.reshape(1, N), res)


def _mlp_kernel(x_ref, g_ref, w1_ref, w2_ref, o_ref, hn_ref):
    f = pl.program_id(1)

    @pl.when(f == 0)
    def _():
        x = x_ref[...]
        ms = jnp.mean(x * x, axis=-1, keepdims=True)
        hn_ref[...] = (x * lax.rsqrt(ms + RMS_EPS) * g_ref[...]).astype(hn_ref.dtype)
        o_ref[...] = x

    a = jnp.maximum(jnp.dot(hn_ref[...], w1_ref[...], preferred_element_type=F32), 0.0)
    a = (a * a).astype(BF16)
    o_ref[...] += jnp.dot(a, w2_ref[...], preferred_element_type=F32)


def _mlp(x, g, w1_stack, w2_stack, layer, tm=512, tf=1024):
    T, D = x.shape
    F = w1_stack.shape[2]
    vmem = 2 * (2 * tm * D * 4 + 2 * D * tf * 2) + tm * D * 2 + tm * tf * 6 + tm * D * 4
    return pl.pallas_call(
        _mlp_kernel,
        out_shape=jax.ShapeDtypeStruct((T, D), F32),
        grid=(T // tm, F // tf),
        in_specs=[pl.BlockSpec((tm, D), lambda i, f: (i, 0)),
                  pl.BlockSpec((1, D), lambda i, f: (0, 0)),
                  pl.BlockSpec((None, D, tf), lambda i, f: (layer, 0, f)),
                  pl.BlockSpec((None, tf, D), lambda i, f: (layer, f, 0))],
        out_specs=pl.BlockSpec((tm, D), lambda i, f: (i, 0)),
        scratch_shapes=[pltpu.VMEM((tm, D), BF16)],
        compiler_params=_params(2, vmem),
        name="mlp",
    )(x, g.reshape(1, D), w1_stack, w2_stack)


def _ple_kernel(x_ref, p_ref, wg_ref, wp_ref, g_ref, *refs, emit_h):
    if emit_h:
        h_ref, y_ref = refs
    else:
        y_ref, h_ref = refs
    xb = x_ref[...].astype(BF16)
    pb = p_ref[...].astype(BF16)
    for c in range(h_ref.shape[1] // COL_CHUNK):
        cs = slice(c * COL_CHUNK, (c + 1) * COL_CHUNK)
        gate = jnp.dot(xb, wg_ref[:, cs], preferred_element_type=F32)
        emb = jnp.dot(pb, wp_ref[:, cs], preferred_element_type=F32)
        h_ref[:, cs] = x_ref[:, cs] + jax.nn.sigmoid(gate) * emb
    h = h_ref[...]
    ms = jnp.mean(h * h, axis=-1, keepdims=True)
    y_ref[...] = (h * lax.rsqrt(ms + RMS_EPS) * g_ref[...]).astype(y_ref.dtype)


def _ple(x, p_stack, wg_stack, wp_stack, layer, g, emit_h, y_dtype, tm=512):
    T, D = x.shape
    P = p_stack.shape[2]
    row = pl.BlockSpec((tm, D), lambda i: (i, 0))
    y_shape = jax.ShapeDtypeStruct((T, D), y_dtype)
    if emit_h:
        out_shape, out_specs, scratch = (jax.ShapeDtypeStruct((T, D), F32), y_shape), (row, row), []
    else:
        out_shape, out_specs, scratch = y_shape, row, [pltpu.VMEM((tm, D), F32)]
    vmem = (2 * (3 * tm * D * 4 + tm * P * 4 + D * D * 2 + P * D * 2)
            + tm * D * 2 + 3 * tm * COL_CHUNK * 4)
    return pl.pallas_call(
        functools.partial(_ple_kernel, emit_h=emit_h),
        out_shape=out_shape,
        grid=(T // tm,),
        in_specs=[row,
                  pl.BlockSpec((None, tm, P), lambda i: (layer, i, 0)),
                  pl.BlockSpec((None, D, D), lambda i: (layer, 0, 0)),
                  pl.BlockSpec((None, P, D), lambda i: (layer, 0, 0)),
                  pl.BlockSpec((1, D), lambda i: (0, 0))],
        out_specs=out_specs,
        scratch_shapes=scratch,
        compiler_params=_params(1, vmem),
        name="ple_norm",
    )(x, p_stack, wg_stack, wp_stack, g.reshape(1, D))


def _glu_kernel(a_ref, wa_ref, wg_ref, ba_ref, bg_ref, o_ref):
    a = a_ref[...]
    lin = jnp.dot(a, wa_ref[...].astype(BF16), preferred_element_type=F32) + ba_ref[...]
    gate = jnp.dot(a, wg_ref[...].astype(BF16), preferred_element_type=F32) + bg_ref[...]
    o_ref[...] = lin * jax.nn.sigmoid(gate)


def _glu(a, w_stack, layer, b, tm=1024, tn=512):
    T, K = a.shape
    N = w_stack.shape[2] // 2
    nj = N // tn
    vmem = 2 * (tm * K * 2 + 2 * K * tn * 4 + tm * tn * 4) + 2 * K * tn * 2 + 3 * tm * tn * 4
    return pl.pallas_call(
        _glu_kernel,
        out_shape=jax.ShapeDtypeStruct((T, N), F32),
        grid=(T // tm, nj),
        in_specs=[pl.BlockSpec((tm, K), lambda i, j: (i, 0)),
                  pl.BlockSpec((None, K, tn), lambda i, j: (layer, 0, j)),
                  pl.BlockSpec((None, K, tn), lambda i, j: (layer, 0, j + nj)),
                  pl.BlockSpec((1, tn), lambda i, j: (0, j)),
                  pl.BlockSpec((1, tn), lambda i, j: (0, j + nj))],
        out_specs=pl.BlockSpec((tm, tn), lambda i, j: (i, j)),
        compiler_params=_params(2, vmem),
        name="pw1_glu",
    )(a, w_stack, w_stack, b.reshape(1, 2 * N), b.reshape(1, 2 * N))


CONV_HALO = 32


def _conv_ln_kernel(u_ref, halo_ref, w_ref, b_ref, g_ref, beta_ref, o_ref, ubuf_ref, cbuf_ref,
                    sbuf_ref):
    n_chunks, tm, _ = cbuf_ref.shape
    at_start = pl.program_id(1) == 0
    for c in range(n_chunks):
        cols = slice(c * LANES, (c + 1) * LANES)
        ubuf_ref[c, 0:CONV_HALO, :] = jnp.where(at_start, 0.0, halo_ref[0, :, cols])
        ubuf_ref[c, CONV_HALO:, :] = u_ref[0, :, cols]

    first = CONV_HALO - (CONV_W - 1)

    def chunk_body(c, carry):
        acc = jnp.zeros((tm, LANES), F32)
        for phase in range(SUBLANES):
            taps = [k for k in range(CONV_W) if (first + k) % SUBLANES == phase]
            span = max(first + k for k in taps) - phase + tm
            if phase > 0:
                sbuf_ref[0:span, :] = ubuf_ref[c, phase:phase + span, :]
            for k in taps:
                off = first + k - phase
                rows = ubuf_ref[c, off:off + tm, :] if phase == 0 else sbuf_ref[off:off + tm, :]
                acc = acc + rows * w_ref[c, k:k + 1, :]
        cbuf_ref[c] = acc
        return carry

    lax.fori_loop(0, n_chunks, chunk_body, 0)
    y = jnp.concatenate([cbuf_ref[c] for c in range(n_chunks)], axis=1) + b_ref[...]
    mu = jnp.mean(y, axis=-1, keepdims=True)
    yc = y - mu
    var = jnp.mean(yc * yc, axis=-1, keepdims=True)
    yn = yc * lax.rsqrt(var + LN_EPS) * g_ref[...] + beta_ref[...]
    o_ref[0] = (yn * jax.nn.sigmoid(yn)).astype(o_ref.dtype)


def _conv_ln(u, w_dw, b_dw, ln_g, ln_b, tm=128):
    B, S, D = u.shape
    per = tm // CONV_HALO
    n_chunks = D // LANES
    w_pad = jnp.zeros((CONV_HALO, D), F32).at[:CONV_W].set(w_dw)
    w_chunks = w_pad.reshape(CONV_HALO, n_chunks, LANES).transpose(1, 0, 2)
    vec = pl.BlockSpec((1, D), lambda b, i: (0, 0))
    vmem = (2 * (tm * D * 4 + 2 * CONV_HALO * D * 4 + tm * D * 2)
            + (2 * tm + CONV_HALO) * D * 4 + (tm + CONV_HALO) * LANES * 4 + 4 * tm * D * 4)
    return pl.pallas_call(
        _conv_ln_kernel,
        out_shape=jax.ShapeDtypeStruct((B, S, D), BF16),
        grid=(B, S // tm),
        in_specs=[pl.BlockSpec((1, tm, D), lambda b, i: (b, i, 0)),
                  pl.BlockSpec((1, CONV_HALO, D), lambda b, i: (b, jnp.maximum(i * per - 1, 0), 0)),
                  pl.BlockSpec((n_chunks, CONV_HALO, LANES), lambda b, i: (0, 0, 0)),
                  vec, vec, vec],
        out_specs=pl.BlockSpec((1, tm, D), lambda b, i: (b, i, 0)),
        scratch_shapes=[pltpu.VMEM((n_chunks, tm + CONV_HALO, LANES), F32),
                        pltpu.VMEM((n_chunks, tm, LANES), F32),
                        pltpu.VMEM((tm + CONV_HALO, LANES), F32)],
        compiler_params=_params(2, vmem),
        name="dwconv_ln_silu",
    )(u, u, w_chunks, b_dw.reshape(1, D), ln_g.reshape(1, D), ln_b.reshape(1, D))


def kernel(x, p, positions, norm_mix_g, norm_mlp_g, final_g, a_w_in, a_w_out, a_kidx_g, a_kidx_b,
           b_w_pw1, b_b_pw1, b_w_dw, b_b_dw, b_ln_g, b_ln_b, b_w_pw2, b_b_pw2,
           mlp_w1, mlp_w2, ple_w_proj, ple_w_gate):
    B, S, D = x.shape
    T = B * S
    depth = norm_mix_g.shape[0]
    h = x.reshape(T, D)
    tabs = _rope_tables(positions.reshape(T, 1).astype(I32))

    q_dim = a_w_out.shape[1]
    idxq_dim = IDX_HEADS * IDX_DIM
    n_in = a_w_in.shape[2]
    kv_dim = (n_in - q_dim - idxq_dim - IDX_DIM - IDX_HEADS) // 2

    w1_bf = mlp_w1.astype(BF16)
    w2_bf = mlp_w2.astype(BF16)
    win_bf = jnp.pad(a_w_in.astype(BF16), ((0, 0), (0, 0), (0, -n_in % PROJ_CHUNK)))
    wout_bf = a_w_out.astype(BF16)
    wpw2_bf = b_w_pw2.astype(BF16)
    wgate_bf = ple_w_gate.astype(BF16)
    wproj_bf = ple_w_proj.astype(BF16)
    p_flat = p.reshape(depth, T, p.shape[-1])
    for i in range(depth):
        j = i // 2
        if i % 2 == 0:
            q, k, vt, qi, ki, wt = _in_proj(h, norm_mix_g[i], win_bf, j, tabs, a_kidx_g[j], a_kidx_b[j],
                                            q_dim, kv_dim, idxq_dim)
            attn = _dsa_attention(q, qi, wt, ki, k, vt, B, S)
            h = _mm_res(attn, wout_bf, j, jnp.zeros((D,), F32), h)
        else:
            u = _glu(hn, b_w_pw1, j, b_b_pw1[j])
            c = _conv_ln(u.reshape(B, S, D), b_w_dw[j], b_b_dw[j], b_ln_g[j], b_ln_b[j])
            h = _mm_res(c.reshape(T, D), wpw2_bf, j, b_b_pw2[j], h)
        h = _mlp(h, norm_mlp_g[i], w1_bf, w2_bf, i)
        if i + 1 < depth:
            h, hn = _ple(h, p_flat, wgate_bf, wproj_bf, i, norm_mix_g[i + 1], True, BF16)
        else:
            out = _ple(h, p_flat, wgate_bf, wproj_bf, i, final_g, False, F32)
    return out.reshape(B, S, D)
```

```python
import functools

import numpy as np
import jax
import jax.numpy as jnp
from jax import lax
from jax.experimental import pallas as pl
from jax.experimental.pallas import tpu as pltpu

F32 = jnp.float32
BF16 = jnp.bfloat16
I32 = jnp.int32

CHUNK = 64
QBLK = 128
HEAD_DIM = 128
GROUP = 4
IDX_HEADS = 16
IDX_DIM = 128
IDX_ROPE = 64
TOPK_MAX = 256
ROPE_THETA = 10000.0
CONV_W = 31
RMS_EPS = 1e-6
LN_EPS = 1e-5

LANES = 128
SUBLANES = 8
V7X_VMEM_BYTES = 64 * 1024 * 1024
VMEM_CAP = V7X_VMEM_BYTES - 8 * 1024 * 1024

INT_MIN = -2 ** 31
MASK_NEG = -1e30
LOG2_E = 1.4426950408889634
CHUNK_SHIFT = CHUNK.bit_length() - 1
NT_DIMS = (((1,), (1,)), ((), ()))


def _params(ndims, vmem_estimate):
    limit = min(int(vmem_estimate * 1.25) + (4 << 20), VMEM_CAP)
    return pltpu.CompilerParams(dimension_semantics=("arbitrary",) * ndims,
                                vmem_limit_bytes=limit)


def _rope_tables_kernel(pos_ref, inv_ref, sgn_ref, c128_ref, s128_ref, c64_ref, s64a_ref, s64b_ref):
    pos = pos_ref[...].astype(F32)
    ang = pos * inv_ref[0:1, :]
    c128_ref[...] = jnp.cos(ang)
    s128_ref[...] = jnp.sin(ang) * sgn_ref[0:1, :]
    ang = pos * inv_ref[1:2, :]
    c64_ref[...] = jnp.cos(ang)
    sn = jnp.sin(ang)
    s64a_ref[...] = sn * sgn_ref[1:2, :]
    s64b_ref[...] = sn * sgn_ref[2:3, :]


def _rope_tables(pos, tm=1024):
    T = pos.shape[0]
    half = HEAD_DIM // 2
    inv_h = ROPE_THETA ** (-jnp.arange(half, dtype=F32) * (2.0 / HEAD_DIM))
    half_i = IDX_ROPE // 2
    inv_i = ROPE_THETA ** (-jnp.arange(half_i, dtype=F32) * (2.0 / IDX_ROPE))
    inv = jnp.zeros((8, LANES), F32)
    inv = inv.at[0].set(jnp.concatenate([inv_h, inv_h]))
    inv = inv.at[1, :IDX_ROPE].set(jnp.concatenate([inv_i, inv_i]))
    sgn = np.zeros((8, LANES), np.float32)
    sgn[0, :half] = -1.0
    sgn[0, half:] = 1.0
    sgn[1, half_i:IDX_ROPE] = 1.0
    sgn[2, :half_i] = -1.0
    tab = jax.ShapeDtypeStruct((T, LANES), F32)
    row = pl.BlockSpec((tm, LANES), lambda i: (i, 0))
    cst = pl.BlockSpec((8, LANES), lambda i: (0, 0))
    return pl.pallas_call(
        _rope_tables_kernel,
        out_shape=(tab,) * 5,
        grid=(T // tm,),
        in_specs=[pl.BlockSpec((tm, 1), lambda i: (i, 0)), cst, cst],
        out_specs=(row,) * 5,
        compiler_params=_params(1, 16 * tm * LANES * 4),
        name="rope_tables",
    )(pos, inv, jnp.asarray(sgn))


PROJ_CHUNK = 512
IDX_ROWS = 256


def _rope_full(yh, c, s):
    return yh * c + pltpu.roll(yh, HEAD_DIM // 2, 1) * s


def _rope_partial(yh, c, sa, sb):
    return (yh * c + pltpu.roll(yh, IDX_ROPE // 2, 1) * sa
            + pltpu.roll(yh, IDX_DIM - IDX_ROPE // 2, 1) * sb)


def _in_proj_kernel(x_ref, g_ref, w_ref, c128_ref, s128_ref, c64_ref, s64a_ref, s64b_ref,
                    lng_ref, lnb_ref, q_ref, k_ref, vt_ref, qi_ref, ki_ref, wt_ref, hn_ref, y_ref,
                    *, n_q, n_qi, q_scale, w_scale):
    j = pl.program_id(1)
    tm = y_ref.shape[0]
    j_k, j_v, j_qi, j_ki = n_q, n_q + 1, n_q + 2, n_q + 2 + n_qi

    def multiply():
        y_ref[...] = jnp.dot(hn_ref[...], w_ref[...], preferred_element_type=F32)

    def head(h):
        return y_ref[:, h * HEAD_DIM:(h + 1) * HEAD_DIM]

    n_heads = PROJ_CHUNK // HEAD_DIM

    @pl.when(j == 0)
    def _():
        x = x_ref[...]
        ms = jnp.mean(x * x, axis=-1, keepdims=True)
        hn_ref[...] = (x * lax.rsqrt(ms + RMS_EPS) * g_ref[...]).astype(hn_ref.dtype)
        multiply()

    @pl.when((j >= 1) & (j <= j_k))
    def _():
        for h in range(n_heads):
            out = _rope_full(head(h), c128_ref[...], s128_ref[...]) * q_scale
            q_ref[:, h] = out.astype(q_ref.dtype).reshape(tm // QBLK, QBLK, HEAD_DIM)
        multiply()

    @pl.when(j == j_k + 1)
    def _():
        for h in range(n_heads):
            out = _rope_full(head(h), c128_ref[...], s128_ref[...])
            k_ref[:, h * HEAD_DIM:(h + 1) * HEAD_DIM] = out.astype(k_ref.dtype)
        multiply()

    @pl.when(j == j_v + 1)
    def _():
        yt = y_ref[...].T
        for c in range(vt_ref.shape[0]):
            vt_ref[c] = yt[:, c * IDX_ROWS:(c + 1) * IDX_ROWS].astype(vt_ref.dtype)
        multiply()

    @pl.when((j >= j_qi + 1) & (j <= j_ki))
    def _():
        for h in range(n_heads):
            out = _rope_partial(head(h), c64_ref[...], s64a_ref[...], s64b_ref[...])
            qi_ref[:, h] = out.astype(qi_ref.dtype).reshape(tm // QBLK, QBLK, HEAD_DIM)
        multiply()

    @pl.when(j == j_ki + 1)
    def _():
        yk = head(0)
        mu = jnp.mean(yk, axis=-1, keepdims=True)
        yc = yk - mu
        var = jnp.mean(yc * yc, axis=-1, keepdims=True)
        yn = yc * lax.rsqrt(var + LN_EPS) * lng_ref[...] + lnb_ref[...]
        ki_ref[...] = _rope_partial(yn, c64_ref[...], s64a_ref[...], s64b_ref[...]).astype(ki_ref.dtype)
        wt_ref[...] = head(1).T[0:IDX_HEADS, :] * w_scale


def _in_proj(x, g, w_stack, layer, tabs, ln_g, ln_b, q_dim, kv_dim, idxq_dim, tm=1024):
    T, D = x.shape
    assert q_dim % PROJ_CHUNK == 0 and idxq_dim % PROJ_CHUNK == 0 and kv_dim == PROJ_CHUNK
    assert IDX_DIM == HEAD_DIM and IDX_DIM + IDX_HEADS <= 2 * HEAD_DIM
    n_q, n_qi = q_dim // PROJ_CHUNK, idxq_dim // PROJ_CHUNK
    n_steps = n_q + 2 + n_qi + 1
    assert w_stack.shape[2] == n_steps * PROJ_CHUNK
    hpc = PROJ_CHUNK // HEAD_DIM
    tab_spec = pl.BlockSpec((tm, LANES), lambda i, j: (i, 0))
    vec_spec = pl.BlockSpec((1, IDX_DIM), lambda i, j: (0, 0))
    hm_block = (tm // QBLK, hpc, QBLK, HEAD_DIM)
    out_shape = (
        jax.ShapeDtypeStruct((T // QBLK, q_dim // HEAD_DIM, QBLK, HEAD_DIM), BF16),
        jax.ShapeDtypeStruct((T, kv_dim), BF16),
        jax.ShapeDtypeStruct((T // IDX_ROWS, kv_dim, IDX_ROWS), BF16),
        jax.ShapeDtypeStruct((T // QBLK, idxq_dim // HEAD_DIM, QBLK, HEAD_DIM), BF16),
        jax.ShapeDtypeStruct((T, IDX_DIM), BF16),
        jax.ShapeDtypeStruct((IDX_HEADS, T), F32))
    out_specs = (
        pl.BlockSpec(hm_block, lambda i, j: (i, jnp.clip(j - 1, 0, n_q - 1), 0, 0)),
        pl.BlockSpec((tm, kv_dim), lambda i, j: (i, 0)),
        pl.BlockSpec((tm // IDX_ROWS, kv_dim, IDX_ROWS), lambda i, j: (i, 0, 0)),
        pl.BlockSpec(hm_block, lambda i, j: (i, jnp.clip(j - 1 - (n_q + 2), 0, n_qi - 1), 0, 0)),
        pl.BlockSpec((tm, IDX_DIM), lambda i, j: (i, 0)),
        pl.BlockSpec((IDX_HEADS, tm), lambda i, j: (0, i)))
    vmem = (2 * (tm * D * 4 + D * PROJ_CHUNK * 2 + 5 * tm * LANES * 4 + 4 * tm * PROJ_CHUNK * 2)
            + tm * D * 2 + 6 * tm * PROJ_CHUNK * 4)
    return pl.pallas_call(
        functools.partial(_in_proj_kernel, n_q=n_q, n_qi=n_qi,
                          q_scale=HEAD_DIM ** -0.5 * LOG2_E,
                          w_scale=IDX_HEADS ** -0.5 * IDX_DIM ** -0.5),
        out_shape=out_shape,
        grid=(T // tm, n_steps + 1),
        in_specs=[pl.BlockSpec((tm, D), lambda i, j: (i, 0)),
                  pl.BlockSpec((1, D), lambda i, j: (0, 0)),
                  pl.BlockSpec((None, D, PROJ_CHUNK),
                               lambda i, j: (layer, 0, jnp.minimum(j, n_steps - 1))),
                  tab_spec, tab_spec, tab_spec, tab_spec, tab_spec, vec_spec, vec_spec],
        out_specs=out_specs,
        scratch_shapes=[pltpu.VMEM((tm, D), BF16), pltpu.VMEM((tm, PROJ_CHUNK), F32)],
        compiler_params=_params(2, vmem),
        name="dsa_in_proj",
    )(x, g.reshape(1, D), w_stack, *tabs, ln_g.reshape(1, IDX_DIM), ln_b.reshape(1, IDX_DIM))


def _sortable_key(x):
    b = lax.bitcast_convert_type(x, I32)
    return b ^ ((b >> 31) & 0x7FFFFFFF)


def _dsa_kernel(q_ref, qi_ref, wt_ref, ki_ref, k_ref, vt_ref, eye_ref, o_ref,
                key_ref, bias_ref, m_ref, l_ref, alpha_ref, acc_ref, s_ref, p_ref,
                *, k_top):
    i = pl.program_id(1)
    seq = ki_ref.shape[0]
    n_heads = q_ref.shape[1]
    R = IDX_ROWS
    nch = (i * QBLK + QBLK + R - 1) // R

    q_glob = i * QBLK + lax.broadcasted_iota(I32, (1, LANES), 1)
    adm_limit = ((q_glob >> CHUNK_SHIFT) + 1) << CHUNK_SHIFT

    qi_all = qi_ref[0].reshape(IDX_HEADS * QBLK, IDX_DIM)

    def idx_chunk(c):
        r0 = pl.multiple_of(c * R, R)
        kib = ki_ref[pl.ds(r0, R), :]
        d = lax.dot_general(kib, qi_all, NT_DIMS, preferred_element_type=F32)
        acc = jnp.zeros((R, LANES), F32)
        for h in range(IDX_HEADS):
            acc = acc + jnp.maximum(d[:, h * QBLK:(h + 1) * QBLK], 0.0) * wt_ref[h:h + 1, :]
        row = r0 + lax.broadcasted_iota(I32, (R, LANES), 0)
        key_ref[pl.ds(r0, R), :] = jnp.where(row < adm_limit, _sortable_key(acc), INT_MIN)

    def idx_pair(pair, carry):
        idx_chunk(2 * pair)
        idx_chunk(2 * pair + 1)
        return carry

    lax.fori_loop(0, nch // 2, idx_pair, 0)

    nch2 = (nch + 1) // 2

    @pl.when(nch % 2 == 1)
    def _():
        idx_chunk(nch - 1)
        r0 = pl.multiple_of(nch * R, R)
        key_ref[pl.ds(r0, R), :] = jnp.full((R, LANES), INT_MIN, I32)

    def count(pred):
        def body(c, cnt):
            r0 = pl.multiple_of(c * R, R)
            blk = key_ref[pl.ds(r0, R), :]
            row = r0 + lax.broadcasted_iota(I32, (R, LANES), 0)
            return cnt + jnp.sum(pred(blk, row).reshape(R // 8, 8, LANES), axis=0)
        cnt = lax.fori_loop(0, nch, body, jnp.zeros((8, LANES), I32))
        return jnp.sum(cnt, axis=0, keepdims=True)

    def count_ge(cand):
        def body(c, cnt):
            r0 = pl.multiple_of(c * 2 * R, 2 * R)
            hit = jnp.where(key_ref[pl.ds(r0, 2 * R), :] >= cand, 1, 0)
            parts = [hit[t * 8:(t + 1) * 8, :] for t in range(2 * R // 8)]
            while len(parts) > 1:
                parts = [a + b for a, b in zip(parts[0::2], parts[1::2])]
            return cnt + parts[0]
        cnt = lax.fori_loop(0, nch2, body, jnp.zeros((8, LANES), I32))
        return jnp.sum(cnt, axis=0, keepdims=True)

    def thr_pass(b, lo):
        cand = lo + jnp.left_shift(jnp.int32(1), 31 - b)
        return jnp.where(count_ge(cand) >= k_top, cand, lo)

    thr = lax.fori_loop(0, 32, thr_pass, jnp.full((1, LANES), INT_MIN, I32))

    c_gt = count(lambda key, row: jnp.where(key > thr, 1, 0))
    c_eq = count(lambda key, row: jnp.where(key == thr, 1, 0))
    need = k_top - c_gt
    n_bits = int(seq).bit_length()

    def tie_rows():
        def tie_pass(b, jlim):
            cand = jlim + jnp.left_shift(jnp.int32(1), n_bits - 1 - b)
            cnt = count(lambda key, row: jnp.where(key == thr, jnp.where(row < cand, 1, 0), 0))
            return jnp.where(cnt <= need, cand, jlim)
        return lax.fori_loop(0, n_bits, tie_pass, jnp.zeros((1, LANES), I32))

    has_excess_ties = jnp.max(jnp.where(c_eq > need, 1, 0)) > 0
    jlim = lax.cond(has_excess_ties, tie_rows, lambda: jnp.full((1, LANES), 2 * seq, I32))

    def bias_body(c, carry):
        r0 = pl.multiple_of(c * R, R)
        key = key_ref[pl.ds(r0, R), :]
        row = r0 + lax.broadcasted_iota(I32, (R, LANES), 0)
        tie_ok = jnp.where(row < jlim, 0.0, MASK_NEG)
        val = jnp.where(key > thr, 0.0, jnp.where(key == thr, tie_ok, MASK_NEG))
        bias_ref[pl.ds(r0, R), :] = jnp.where(row < adm_limit, val, MASK_NEG).astype(BF16)
        return carry

    lax.fori_loop(0, nch, bias_body, 0)

    n_kv = n_heads // GROUP
    cols = GROUP * QBLK
    m_ref[...] = jnp.full(m_ref.shape, -3e38, F32)
    l_ref[...] = jnp.zeros(l_ref.shape, F32)
    acc_ref[...] = jnp.zeros(acc_ref.shape, F32)

    def scores(c, slot):
        r0 = pl.multiple_of(c * R, R)
        mask = bias_ref[pl.ds(r0, R), :]
        for g in range(n_kv):
            qg = q_ref[0, g * GROUP:(g + 1) * GROUP].reshape(cols, HEAD_DIM)
            kblk = k_ref[pl.ds(r0, R), g * HEAD_DIM:(g + 1) * HEAD_DIM]
            s = lax.dot_general(jnp.concatenate([kblk, mask], axis=1),
                                jnp.concatenate([qg, eye_ref[...]], axis=1),
                                NT_DIMS, preferred_element_type=F32)
            s_ref[slot, g] = s
            m_old = m_ref[1 - slot, g]
            m_new = jnp.maximum(m_old, jnp.max(s, axis=0, keepdims=True))
            alpha_ref[slot, g] = jnp.exp2(m_old - m_new)
            m_ref[slot, g] = m_new

    def accumulate(c, slot):
        for g in range(n_kv):
            p = jnp.exp2(s_ref[slot, g] - m_ref[slot, g][0:1])
            l_ref[g] = alpha_ref[slot, g] * l_ref[g] + jnp.sum(p, axis=0, keepdims=True)
            p_ref[g] = p.astype(BF16)
        for g in range(n_kv):
            vt = vt_ref[c, g * HEAD_DIM:(g + 1) * HEAD_DIM, :]
            acc_ref[g] = alpha_ref[slot, g][0:1] * acc_ref[g] + jnp.dot(
                vt, p_ref[g], preferred_element_type=F32)

    scores(0, 0)

    def pair_body(pair, carry):
        c = 2 * pair
        scores(c + 1, 1)
        accumulate(c, 0)
        scores(c + 2, 0)
        accumulate(c + 1, 1)
        return carry

    lax.fori_loop(0, (nch - 1) // 2, pair_body, 0)

    @pl.when(nch % 2 == 0)
    def _():
        scores(nch - 1, 1)
        accumulate(nch - 2, 0)
        accumulate(nch - 1, 1)

    @pl.when(nch % 2 == 1)
    def _():
        accumulate(nch - 1, 0)

    for g in range(n_kv):
        out_t = acc_ref[g] / l_ref[g][0:1]
        for r in range(GROUP):
            h = g * GROUP + r
            o_ref[:, h * HEAD_DIM:(h + 1) * HEAD_DIM] = (
                out_t[:, r * QBLK:(r + 1) * QBLK].T.astype(o_ref.dtype))


def _dsa_attention(q, qi, wt, ki, k, vt, batch, seq):
    T = batch * seq
    nq = seq // QBLK
    n_heads = q.shape[1]
    kv_dim = k.shape[1]
    n_kv = n_heads // GROUP
    k_top = min(TOPK_MAX, seq // 4)
    cols = GROUP * QBLK
    vmem = (2 * (2 * n_heads * QBLK * HEAD_DIM * 2 + seq * IDX_DIM * 2 + 2 * seq * kv_dim * 2
                 + QBLK * n_heads * HEAD_DIM * 2)
            + 2 * seq * LANES * 4 + n_kv * (24 + HEAD_DIM) * cols * 4
            + IDX_ROWS * IDX_HEADS * QBLK * 4 + 4 * n_kv * cols * IDX_ROWS * 4)
    hm_spec = pl.BlockSpec((1, n_heads, QBLK, HEAD_DIM), lambda b, i: (b * nq + i, 0, 0, 0))
    return pl.pallas_call(
        functools.partial(_dsa_kernel, k_top=k_top),
        out_shape=jax.ShapeDtypeStruct((T, n_heads * HEAD_DIM), BF16),
        grid=(batch, nq),
        in_specs=[hm_spec, hm_spec,
                  pl.BlockSpec((IDX_HEADS, QBLK), lambda b, i: (0, b * nq + i)),
                  pl.BlockSpec((seq, IDX_DIM), lambda b, i: (b, 0)),
                  pl.BlockSpec((seq, kv_dim), lambda b, i: (b, 0)),
                  pl.BlockSpec((seq // IDX_ROWS, kv_dim, IDX_ROWS), lambda b, i: (b, 0, 0)),
                  pl.BlockSpec((cols, QBLK), lambda b, i: (0, 0))],
        out_specs=pl.BlockSpec((QBLK, n_heads * HEAD_DIM), lambda b, i: (b * nq + i, 0)),
        scratch_shapes=[pltpu.VMEM((seq, LANES), I32),
                        pltpu.VMEM((seq, LANES), BF16),
                        pltpu.VMEM((2, n_kv, 8, cols), F32),
                        pltpu.VMEM((n_kv, 8, cols), F32),
                        pltpu.VMEM((2, n_kv, 8, cols), F32),
                        pltpu.VMEM((n_kv, HEAD_DIM, cols), F32),
                        pltpu.VMEM((2, n_kv, IDX_ROWS, cols), F32),
                        pltpu.VMEM((n_kv, IDX_ROWS, cols), BF16)],
        compiler_params=_params(2, vmem),
        name="dsa_attention",
    )(q, qi, wt, ki, k, vt, jnp.tile(jnp.eye(QBLK, dtype=BF16), (GROUP, 1)))


COL_CHUNK = 512


def _mm_res_kernel(a_ref, w_ref, b_ref, r_ref, o_ref):
    a = a_ref[...]
    for c in range(o_ref.shape[1] // COL_CHUNK):
        cs = slice(c * COL_CHUNK, (c + 1) * COL_CHUNK)
        y = jnp.dot(a, w_ref[:, cs], preferred_element_type=F32)
        o_ref[:, cs] = r_ref[:, cs] + (y + b_ref[:, cs])


def _mm_res(a, w_stack, layer, bias, res, tm=512):
    T, K = a.shape
    N = w_stack.shape[2]
    vmem = 2 * (tm * K * 2 + K * N * 2 + 2 * tm * N * 4) + 3 * tm * COL_CHUNK * 4
    return pl.pallas_call(
        _mm_res_kernel,
        out_shape=jax.ShapeDtypeStruct((T, N), F32),
        grid=(T // tm,),
        in_specs=[pl.BlockSpec((tm, K), lambda i: (i, 0)),
                  pl.BlockSpec((None, K, N), lambda i: (layer, 0, 0)),
                  pl.BlockSpec((1, N), lambda i: (0, 0)),
                  pl.BlockSpec((tm, N), lambda i: (i, 0))],
        out_specs=pl.BlockSpec((tm, N), lambda i: (i, 0)),
        compiler_params=_params(1, vmem),
        name="matmul_residual",
    )(a, w_stack, bias.reshape(1, N), res)


def _mlp_kernel(x_ref, g_ref, w1_ref, w2_ref, o_ref, hn_ref):
    f = pl.program_id(1)

    @pl.when(f == 0)
    def _():
        x = x_ref[...]
        ms = jnp.mean(x * x, axis=-1, keepdims=True)
        hn_ref[...] = (x * lax.rsqrt(ms + RMS_EPS) * g_ref[...]).astype(hn_ref.dtype)
        o_ref[...] = x

    a = jnp.maximum(jnp.dot(hn_ref[...], w1_ref[...], preferred_element_type=F32), 0.0)
    a = (a * a).astype(BF16)
    o_ref[...] += jnp.dot(a, w2_ref[...], preferred_element_type=F32)


def _mlp(x, g, w1_stack, w2_stack, layer, tm=512, tf=1024):
    T, D = x.shape
    F = w1_stack.shape[2]
    vmem = 2 * (2 * tm * D * 4 + 2 * D * tf * 2) + tm * D * 2 + tm * tf * 6 + tm * D * 4
    return pl.pallas_call(
        _mlp_kernel,
        out_shape=jax.ShapeDtypeStruct((T, D), F32),
        grid=(T // tm, F // tf),
        in_specs=[pl.BlockSpec((tm, D), lambda i, f: (i, 0)),
                  pl.BlockSpec((1, D), lambda i, f: (0, 0)),
                  pl.BlockSpec((None, D, tf), lambda i, f: (layer, 0, f)),
                  pl.BlockSpec((None, tf, D), lambda i, f: (layer, f, 0))],
        out_specs=pl.BlockSpec((tm, D), lambda i, f: (i, 0)),
        scratch_shapes=[pltpu.VMEM((tm, D), BF16)],
        compiler_params=_params(2, vmem),
        name="mlp",
    )(x, g.reshape(1, D), w1_stack, w2_stack)


def _ple_kernel(x_ref, p_ref, wg_ref, wp_ref, g_ref, *refs, emit_h):
    if emit_h:
        h_ref, y_ref = refs
    else:
        y_ref, h_ref = refs
    xb = x_ref[...].astype(BF16)
    pb = p_ref[...].astype(BF16)
    for c in range(h_ref.shape[1] // COL_CHUNK):
        cs = slice(c * COL_CHUNK, (c + 1) * COL_CHUNK)
        gate = jnp.dot(xb, wg_ref[:, cs], preferred_element_type=F32)
        emb = jnp.dot(pb, wp_ref[:, cs], preferred_element_type=F32)
        h_ref[:, cs] = x_ref[:, cs] + jax.nn.sigmoid(gate) * emb
    h = h_ref[...]
    ms = jnp.mean(h * h, axis=-1, keepdims=True)
    y_ref[...] = (h * lax.rsqrt(ms + RMS_EPS) * g_ref[...]).astype(y_ref.dtype)


def _ple(x, p_stack, wg_stack, wp_stack, layer, g, emit_h, y_dtype, tm=512):
    T, D = x.shape
    P = p_stack.shape[2]
    row = pl.BlockSpec((tm, D), lambda i: (i, 0))
    y_shape = jax.ShapeDtypeStruct((T, D), y_dtype)
    if emit_h:
        out_shape, out_specs, scratch = (jax.ShapeDtypeStruct((T, D), F32), y_shape), (row, row), []
    else:
        out_shape, out_specs, scratch = y_shape, row, [pltpu.VMEM((tm, D), F32)]
    vmem = (2 * (3 * tm * D * 4 + tm * P * 4 + D * D * 2 + P * D * 2)
            + tm * D * 2 + 3 * tm * COL_CHUNK * 4)
    return pl.pallas_call(
        functools.partial(_ple_kernel, emit_h=emit_h),
        out_shape=out_shape,
        grid=(T // tm,),
        in_specs=[row,
                  pl.BlockSpec((None, tm, P), lambda i: (layer, i, 0)),
                  pl.BlockSpec((None, D, D), lambda i: (layer, 0, 0)),
                  pl.BlockSpec((None, P, D), lambda i: (layer, 0, 0)),
                  pl.BlockSpec((1, D), lambda i: (0, 0))],
        out_specs=out_specs,
        scratch_shapes=scratch,
        compiler_params=_params(1, vmem),
        name="ple_norm",
    )(x, p_stack, wg_stack, wp_stack, g.reshape(1, D))


def _glu_kernel(a_ref, wa_ref, wg_ref, ba_ref, bg_ref, o_ref):
    a = a_ref[...]
    lin = jnp.dot(a, wa_ref[...].astype(BF16), preferred_element_type=F32) + ba_ref[...]
    gate = jnp.dot(a, wg_ref[...].astype(BF16), preferred_element_type=F32) + bg_ref[...]
    o_ref[...] = lin * jax.nn.sigmoid(gate)


def _glu(a, w_stack, layer, b, tm=1024, tn=512):
    T, K = a.shape
    N = w_stack.shape[2] // 2
    nj = N // tn
    vmem = 2 * (tm * K * 2 + 2 * K * tn * 4 + tm * tn * 4) + 2 * K * tn * 2 + 3 * tm * tn * 4
    return pl.pallas_call(
        _glu_kernel,
        out_shape=jax.ShapeDtypeStruct((T, N), F32),
        grid=(T // tm, nj),
        in_specs=[pl.BlockSpec((tm, K), lambda i, j: (i, 0)),
                  pl.BlockSpec((None, K, tn), lambda i, j: (layer, 0, j)),
                  pl.BlockSpec((None, K, tn), lambda i, j: (layer, 0, j + nj)),
                  pl.BlockSpec((1, tn), lambda i, j: (0, j)),
                  pl.BlockSpec((1, tn), lambda i, j: (0, j + nj))],
        out_specs=pl.BlockSpec((tm, tn), lambda i, j: (i, j)),
        compiler_params=_params(2, vmem),
        name="pw1_glu",
    )(a, w_stack, w_stack, b.reshape(1, 2 * N), b.reshape(1, 2 * N))


CONV_HALO = 32


def _conv_ln_kernel(u_ref, halo_ref, w_ref, b_ref, g_ref, beta_ref, o_ref, ubuf_ref, cbuf_ref,
                    sbuf_ref):
    n_chunks, tm, _ = cbuf_ref.shape
    at_start = pl.program_id(1) == 0
    for c in range(n_chunks):
        cols = slice(c * LANES, (c + 1) * LANES)
        ubuf_ref[c, 0:CONV_HALO, :] = jnp.where(at_start, 0.0, halo_ref[0, :, cols])
        ubuf_ref[c, CONV_HALO:, :] = u_ref[0, :, cols]

    first = CONV_HALO - (CONV_W - 1)

    def chunk_body(c, carry):
        acc = jnp.zeros((tm, LANES), F32)
        for phase in range(SUBLANES):
            taps = [k for k in range(CONV_W) if (first + k) % SUBLANES == phase]
            span = max(first + k for k in taps) - phase + tm
            if phase > 0:
                sbuf_ref[0:span, :] = ubuf_ref[c, phase:phase + span, :]
            for k in taps:
                off = first + k - phase
                rows = ubuf_ref[c, off:off + tm, :] if phase == 0 else sbuf_ref[off:off + tm, :]
                acc = acc + rows * w_ref[c, k:k + 1, :]
        cbuf_ref[c] = acc
        return carry

    lax.fori_loop(0, n_chunks, chunk_body, 0)
    y = jnp.concatenate([cbuf_ref[c] for c in range(n_chunks)], axis=1) + b_ref[...]
    mu = jnp.mean(y, axis=-1, keepdims=True)
    yc = y - mu
    var = jnp.mean(yc * yc, axis=-1, keepdims=True)
    yn = yc * lax.rsqrt(var + LN_EPS) * g_ref[...] + beta_ref[...]
    o_ref[0] = (yn * jax.nn.sigmoid(yn)).astype(o_ref.dtype)


def _conv_ln(u, w_dw, b_dw, ln_g, ln_b, tm=128):
    B, S, D = u.shape
    per = tm // CONV_HALO
    n_chunks = D // LANES
    w_pad = jnp.zeros((CONV_HALO, D), F32).at[:CONV_W].set(w_dw)
    w_chunks = w_pad.reshape(CONV_HALO, n_chunks, LANES).transpose(1, 0, 2)
    vec = pl.BlockSpec((1, D), lambda b, i: (0, 0))
    vmem = (2 * (tm * D * 4 + 2 * CONV_HALO * D * 4 + tm * D * 2)
            + (2 * tm + CONV_HALO) * D * 4 + (tm + CONV_HALO) * LANES * 4 + 4 * tm * D * 4)
    return pl.pallas_call(
        _conv_ln_kernel,
        out_shape=jax.ShapeDtypeStruct((B, S, D), BF16),
        grid=(B, S // tm),
        in_specs=[pl.BlockSpec((1, tm, D), lambda b, i: (b, i, 0)),
                  pl.BlockSpec((1, CONV_HALO, D), lambda b, i: (b, jnp.maximum(i * per - 1, 0), 0)),
                  pl.BlockSpec((n_chunks, CONV_HALO, LANES), lambda b, i: (0, 0, 0)),
                  vec, vec, vec],
        out_specs=pl.BlockSpec((1, tm, D), lambda b, i: (b, i, 0)),
        scratch_shapes=[pltpu.VMEM((n_chunks, tm + CONV_HALO, LANES), F32),
                        pltpu.VMEM((n_chunks, tm, LANES), F32),
                        pltpu.VMEM((tm + CONV_HALO, LANES), F32)],
        compiler_params=_params(2, vmem),
        name="dwconv_ln_silu",
    )(u, u, w_chunks, b_dw.reshape(1, D), ln_g.reshape(1, D), ln_b.reshape(1, D))


def kernel(x, p, positions, norm_mix_g, norm_mlp_g, final_g, a_w_in, a_w_out, a_kidx_g, a_kidx_b,
           b_w_pw1, b_b_pw1, b_w_dw, b_b_dw, b_ln_g, b_ln_b, b_w_pw2, b_b_pw2,
           mlp_w1, mlp_w2, ple_w_proj, ple_w_gate):
    B, S, D = x.shape
    T = B * S
    depth = norm_mix_g.shape[0]
    h = x.reshape(T, D)
    tabs = _rope_tables(positions.reshape(T, 1).astype(I32))

    q_dim = a_w_out.shape[1]
    idxq_dim = IDX_HEADS * IDX_DIM
    n_in = a_w_in.shape[2]
    kv_dim = (n_in - q_dim - idxq_dim - IDX_DIM - IDX_HEADS) // 2

    w1_bf = mlp_w1.astype(BF16)
    w2_bf = mlp_w2.astype(BF16)
    win_bf = jnp.pad(a_w_in.astype(BF16), ((0, 0), (0, 0), (0, -n_in % PROJ_CHUNK)))
    wout_bf = a_w_out.astype(BF16)
    wpw2_bf = b_w_pw2.astype(BF16)
    wgate_bf = ple_w_gate.astype(BF16)
    wproj_bf = ple_w_proj.astype(BF16)
    p_flat = p.reshape(depth, T, p.shape[-1])
    for i in range(depth):
        j = i // 2
        if i % 2 == 0:
            q, k, vt, qi, ki, wt = _in_proj(h, norm_mix_g[i], win_bf, j, tabs, a_kidx_g[j], a_kidx_b[j],
                                            q_dim, kv_dim, idxq_dim)
            attn = _dsa_attention(q, qi, wt, ki, k, vt, B, S)
            h = _mm_res(attn, wout_bf, j, jnp.zeros((D,), F32), h)
        else:
            u = _glu(hn, b_w_pw1, j, b_b_pw1[j])
            c = _conv_ln(u.reshape(B, S, D), b_w_dw[j], b_b_dw[j], b_ln_g[j], b_ln_b[j])
            h = _mm_res(c.reshape(T, D), wpw2_bf, j, b_b_pw2[j], h)
        h = _mlp(h, norm_mlp_g[i], w1_bf, w2_bf, i)
        if i + 1 < depth:
            h, hn = _ple(h, p_flat, wgate_bf, wproj_bf, i, norm_mix_g[i + 1], True, BF16)
        else:
            out = _ple(h, p_flat, wgate_bf, wproj_bf, i, final_g, False, F32)
    return out.reshape(B, S, D)
```

```python
import functools

import numpy as np
import jax
import jax.numpy as jnp
from jax import lax
from jax.experimental import pallas as pl
from jax.experimental.pallas import tpu as pltpu

F32 = jnp.float32
BF16 = jnp.bfloat16
I32 = jnp.int32

CHUNK = 64
QBLK = 128
HEAD_DIM = 128
GROUP = 4
IDX_HEADS = 16
IDX_DIM = 128
IDX_ROPE = 64
TOPK_MAX = 256
ROPE_THETA = 10000.0
CONV_W = 31
RMS_EPS = 1e-6
LN_EPS = 1e-5

LANES = 128
SUBLANES = 8
V7X_VMEM_BYTES = 64 * 1024 * 1024
VMEM_CAP = V7X_VMEM_BYTES - 8 * 1024 * 1024

INT_MIN = -2 ** 31
MASK_NEG = -1e30
LOG2_E = 1.4426950408889634
CHUNK_SHIFT = CHUNK.bit_length() - 1
NT_DIMS = (((1,), (1,)), ((), ()))


def _params(ndims, vmem_estimate):
    limit = min(int(vmem_estimate * 1.25) + (4 << 20), VMEM_CAP)
    return pltpu.CompilerParams(dimension_semantics=("arbitrary",) * ndims,
                                vmem_limit_bytes=limit)


def _rope_tables_kernel(pos_ref, inv_ref, sgn_ref, c128_ref, s128_ref, c64_ref, s64a_ref, s64b_ref):
    pos = pos_ref[...].astype(F32)
    ang = pos * inv_ref[0:1, :]
    c128_ref[...] = jnp.cos(ang)
    s128_ref[...] = jnp.sin(ang) * sgn_ref[0:1, :]
    ang = pos * inv_ref[1:2, :]
    c64_ref[...] = jnp.cos(ang)
    sn = jnp.sin(ang)
    s64a_ref[...] = sn * sgn_ref[1:2, :]
    s64b_ref[...] = sn * sgn_ref[2:3, :]


def _rope_tables(pos, tm=1024):
    T = pos.shape[0]
    half = HEAD_DIM // 2
    inv_h = ROPE_THETA ** (-jnp.arange(half, dtype=F32) * (2.0 / HEAD_DIM))
    half_i = IDX_ROPE // 2
    inv_i = ROPE_THETA ** (-jnp.arange(half_i, dtype=F32) * (2.0 / IDX_ROPE))
    inv = jnp.zeros((8, LANES), F32)
    inv = inv.at[0].set(jnp.concatenate([inv_h, inv_h]))
    inv = inv.at[1, :IDX_ROPE].set(jnp.concatenate([inv_i, inv_i]))
    sgn = np.zeros((8, LANES), np.float32)
    sgn[0, :half] = -1.0
    sgn[0, half:] = 1.0
    sgn[1, half_i:IDX_ROPE] = 1.0
    sgn[2, :half_i] = -1.0
    tab = jax.ShapeDtypeStruct((T, LANES), F32)
    row = pl.BlockSpec((tm, LANES), lambda i: (i, 0))
    cst = pl.BlockSpec((8, LANES), lambda i: (0, 0))
    return pl.pallas_call(
        _rope_tables_kernel,
        out_shape=(tab,) * 5,
        grid=(T // tm,),
        in_specs=[pl.BlockSpec((tm, 1), lambda i: (i, 0)), cst, cst],
        out_specs=(row,) * 5,
        compiler_params=_params(1, 16 * tm * LANES * 4),
        name="rope_tables",
    )(pos, inv, jnp.asarray(sgn))


PROJ_CHUNK = 512
IDX_ROWS = 256
V_ROWS = HEAD_DIM + 16


def _rope_full(yh, c, s):
    return yh * c + pltpu.roll(yh, HEAD_DIM // 2, 1) * s


def _rope_partial(yh, c, sa, sb):
    return (yh * c + pltpu.roll(yh, IDX_ROPE // 2, 1) * sa
            + pltpu.roll(yh, IDX_DIM - IDX_ROPE // 2, 1) * sb)


def _in_proj_kernel(x_ref, g_ref, w_ref, c128_ref, s128_ref, c64_ref, s64a_ref, s64b_ref,
                    lng_ref, lnb_ref, q_ref, k_ref, vt_ref, qi_ref, ki_ref, wt_ref, hn_ref, y_ref,
                    *, n_q, n_qi, q_scale, w_scale):
    j = pl.program_id(1)
    tm = y_ref.shape[0]
    j_k, j_v, j_qi, j_ki = n_q, n_q + 1, n_q + 2, n_q + 2 + n_qi

    def multiply():
        y_ref[...] = jnp.dot(hn_ref[...], w_ref[...], preferred_element_type=F32)

    def head(h):
        return y_ref[:, h * HEAD_DIM:(h + 1) * HEAD_DIM]

    n_heads = PROJ_CHUNK // HEAD_DIM

    @pl.when(j == 0)
    def _():
        x = x_ref[...]
        ms = jnp.mean(x * x, axis=-1, keepdims=True)
        hn_ref[...] = (x * lax.rsqrt(ms + RMS_EPS) * g_ref[...]).astype(hn_ref.dtype)
        multiply()

    @pl.when((j >= 1) & (j <= j_k))
    def _():
        for h in range(n_heads):
            out = _rope_full(head(h), c128_ref[...], s128_ref[...]) * q_scale
            q_ref[:, h] = out.astype(q_ref.dtype).reshape(tm // QBLK, QBLK, HEAD_DIM)
        multiply()

    @pl.when(j == j_k + 1)
    def _():
        for h in range(n_heads):
            out = _rope_full(head(h), c128_ref[...], s128_ref[...])
            k_ref[:, h * HEAD_DIM:(h + 1) * HEAD_DIM] = out.astype(k_ref.dtype)
        multiply()

    @pl.when(j == j_v + 1)
    def _():
        yt = y_ref[...].T
        ones = jnp.ones((V_ROWS - HEAD_DIM, IDX_ROWS), vt_ref.dtype)
        for c in range(vt_ref.shape[0]):
            for g in range(PROJ_CHUNK // HEAD_DIM):
                vt_ref[c, g * V_ROWS:g * V_ROWS + HEAD_DIM, :] = yt[
                    g * HEAD_DIM:(g + 1) * HEAD_DIM, c * IDX_ROWS:(c + 1) * IDX_ROWS].astype(vt_ref.dtype)
                vt_ref[c, g * V_ROWS + HEAD_DIM:(g + 1) * V_ROWS, :] = ones
        multiply()

    @pl.when((j >= j_qi + 1) & (j <= j_ki))
    def _():
        for h in range(n_heads):
            out = _rope_partial(head(h), c64_ref[...], s64a_ref[...], s64b_ref[...])
            qi_ref[:, h] = out.astype(qi_ref.dtype).reshape(tm // QBLK, QBLK, HEAD_DIM)
        multiply()

    @pl.when(j == j_ki + 1)
    def _():
        yk = head(0)
        mu = jnp.mean(yk, axis=-1, keepdims=True)
        yc = yk - mu
        var = jnp.mean(yc * yc, axis=-1, keepdims=True)
        yn = yc * lax.rsqrt(var + LN_EPS) * lng_ref[...] + lnb_ref[...]
        ki_ref[...] = _rope_partial(yn, c64_ref[...], s64a_ref[...], s64b_ref[...]).astype(ki_ref.dtype)
        wt_ref[...] = head(1).T[0:IDX_HEADS, :] * w_scale


def _in_proj(x, g, w_stack, layer, tabs, ln_g, ln_b, q_dim, kv_dim, idxq_dim, tm=1024):
    T, D = x.shape
    assert q_dim % PROJ_CHUNK == 0 and idxq_dim % PROJ_CHUNK == 0 and kv_dim == PROJ_CHUNK
    assert IDX_DIM == HEAD_DIM and IDX_DIM + IDX_HEADS <= 2 * HEAD_DIM
    n_q, n_qi = q_dim // PROJ_CHUNK, idxq_dim // PROJ_CHUNK
    n_steps = n_q + 2 + n_qi + 1
    assert w_stack.shape[2] == n_steps * PROJ_CHUNK
    hpc = PROJ_CHUNK // HEAD_DIM
    tab_spec = pl.BlockSpec((tm, LANES), lambda i, j: (i, 0))
    vec_spec = pl.BlockSpec((1, IDX_DIM), lambda i, j: (0, 0))
    hm_block = (tm // QBLK, hpc, QBLK, HEAD_DIM)
    out_shape = (
        jax.ShapeDtypeStruct((T // QBLK, q_dim // HEAD_DIM, QBLK, HEAD_DIM), BF16),
        jax.ShapeDtypeStruct((T, kv_dim), BF16),
        jax.ShapeDtypeStruct((T // IDX_ROWS, kv_dim // HEAD_DIM * V_ROWS, IDX_ROWS), BF16),
        jax.ShapeDtypeStruct((T // QBLK, idxq_dim // HEAD_DIM, QBLK, HEAD_DIM), BF16),
        jax.ShapeDtypeStruct((T, IDX_DIM), BF16),
        jax.ShapeDtypeStruct((IDX_HEADS, T), F32))
    out_specs = (
        pl.BlockSpec(hm_block, lambda i, j: (i, jnp.clip(j - 1, 0, n_q - 1), 0, 0)),
        pl.BlockSpec((tm, kv_dim), lambda i, j: (i, 0)),
        pl.BlockSpec((tm // IDX_ROWS, kv_dim // HEAD_DIM * V_ROWS, IDX_ROWS), lambda i, j: (i, 0, 0)),
        pl.BlockSpec(hm_block, lambda i, j: (i, jnp.clip(j - 1 - (n_q + 2), 0, n_qi - 1), 0, 0)),
        pl.BlockSpec((tm, IDX_DIM), lambda i, j: (i, 0)),
        pl.BlockSpec((IDX_HEADS, tm), lambda i, j: (0, i)))
    vmem = (2 * (tm * D * 4 + D * PROJ_CHUNK * 2 + 5 * tm * LANES * 4 + 4 * tm * PROJ_CHUNK * 2)
            + tm * D * 2 + 6 * tm * PROJ_CHUNK * 4)
    return pl.pallas_call(
        functools.partial(_in_proj_kernel, n_q=n_q, n_qi=n_qi,
                          q_scale=HEAD_DIM ** -0.5 * LOG2_E,
                          w_scale=IDX_HEADS ** -0.5 * IDX_DIM ** -0.5),
        out_shape=out_shape,
        grid=(T // tm, n_steps + 1),
        in_specs=[pl.BlockSpec((tm, D), lambda i, j: (i, 0)),
                  pl.BlockSpec((1, D), lambda i, j: (0, 0)),
                  pl.BlockSpec((None, D, PROJ_CHUNK),
                               lambda i, j: (layer, 0, jnp.minimum(j, n_steps - 1))),
                  tab_spec, tab_spec, tab_spec, tab_spec, tab_spec, vec_spec, vec_spec],
        out_specs=out_specs,
        scratch_shapes=[pltpu.VMEM((tm, D), BF16), pltpu.VMEM((tm, PROJ_CHUNK), F32)],
        compiler_params=_params(2, vmem),
        name="dsa_in_proj",
    )(x, g.reshape(1, D), w_stack, *tabs, ln_g.reshape(1, IDX_DIM), ln_b.reshape(1, IDX_DIM))


def _sortable_key(x):
    b = lax.bitcast_convert_type(x, I32)
    return b ^ ((b >> 31) & 0x7FFFFFFF)


def _dsa_kernel(q_ref, qi_ref, wt_ref, ki_ref, k_ref, vt_ref, eye_ref, o_ref,
                key_ref, bias_ref, m_ref, alpha_ref, acc_ref, s_ref, p_ref,
                *, k_top):
    i = pl.program_id(1)
    seq = ki_ref.shape[0]
    n_heads = q_ref.shape[1]
    R = IDX_ROWS
    nch = (i * QBLK + QBLK + R - 1) // R

    q_glob = i * QBLK + lax.broadcasted_iota(I32, (1, LANES), 1)
    adm_limit = ((q_glob >> CHUNK_SHIFT) + 1) << CHUNK_SHIFT

    qi_all = qi_ref[0].reshape(IDX_HEADS * QBLK, IDX_DIM)

    def idx_chunk(c):
        r0 = pl.multiple_of(c * R, R)
        kib = ki_ref[pl.ds(r0, R), :]
        d = lax.dot_general(kib, qi_all, NT_DIMS, preferred_element_type=F32)
        acc = jnp.zeros((R, LANES), F32)
        for h in range(IDX_HEADS):
            acc = acc + jnp.maximum(d[:, h * QBLK:(h + 1) * QBLK], 0.0) * wt_ref[h:h + 1, :]
        row = r0 + lax.broadcasted_iota(I32, (R, LANES), 0)
        key_ref[pl.ds(r0, R), :] = jnp.where(row < adm_limit, _sortable_key(acc), INT_MIN)

    def idx_pair(pair, carry):
        idx_chunk(2 * pair)
        idx_chunk(2 * pair + 1)
        return carry

    lax.fori_loop(0, nch // 2, idx_pair, 0)

    nch2 = (nch + 1) // 2

    @pl.when(nch % 2 == 1)
    def _():
        idx_chunk(nch - 1)
        r0 = pl.multiple_of(nch * R, R)
        key_ref[pl.ds(r0, R), :] = jnp.full((R, LANES), INT_MIN, I32)

    def count(pred):
        def body(c, cnt):
            r0 = pl.multiple_of(c * R, R)
            blk = key_ref[pl.ds(r0, R), :]
            row = r0 + lax.broadcasted_iota(I32, (R, LANES), 0)
            return cnt + jnp.sum(pred(blk, row).reshape(R // 8, 8, LANES), axis=0)
        cnt = lax.fori_loop(0, nch, body, jnp.zeros((8, LANES), I32))
        return jnp.sum(cnt, axis=0, keepdims=True)

    def count_ge(cand):
        def body(c, cnt):
            r0 = pl.multiple_of(c * 2 * R, 2 * R)
            hit = jnp.where(key_ref[pl.ds(r0, 2 * R), :] >= cand, 1, 0)
            parts = [hit[t * 8:(t + 1) * 8, :] for t in range(2 * R // 8)]
            while len(parts) > 1:
                parts = [a + b for a, b in zip(parts[0::2], parts[1::2])]
            return cnt + parts[0]
        cnt = lax.fori_loop(0, nch2, body, jnp.zeros((8, LANES), I32))
        return jnp.sum(cnt, axis=0, keepdims=True)

    def thr_pass(b, lo):
        cand = lo + jnp.left_shift(jnp.int32(1), 31 - b)
        return jnp.where(count_ge(cand) >= k_top, cand, lo)

    thr = lax.fori_loop(0, 32, thr_pass, jnp.full((1, LANES), INT_MIN, I32))

    def gt_eq_body(c, carry):
        r0 = pl.multiple_of(c * R, R)
        blk = key_ref[pl.ds(r0, R), :]
        gt = jnp.sum(jnp.where(blk > thr, 1, 0).reshape(R // 8, 8, LANES), axis=0)
        eq = jnp.sum(jnp.where(blk == thr, 1, 0).reshape(R // 8, 8, LANES), axis=0)
        return carry[0] + gt, carry[1] + eq

    zero = jnp.zeros((8, LANES), I32)
    c_gt, c_eq = [jnp.sum(v, axis=0, keepdims=True)
                  for v in lax.fori_loop(0, nch, gt_eq_body, (zero, zero))]
    need = k_top - c_gt
    n_bits = int(seq).bit_length()

    def tie_rows():
        def tie_pass(b, jlim):
            cand = jlim + jnp.left_shift(jnp.int32(1), n_bits - 1 - b)
            cnt = count(lambda key, row: jnp.where(key == thr, jnp.where(row < cand, 1, 0), 0))
            return jnp.where(cnt <= need, cand, jlim)
        return lax.fori_loop(0, n_bits, tie_pass, jnp.zeros((1, LANES), I32))

    has_excess_ties = jnp.max(jnp.where(c_eq > need, 1, 0)) > 0
    jlim = lax.cond(has_excess_ties, tie_rows, lambda: jnp.full((1, LANES), 2 * seq, I32))

    def bias_body(c, carry):
        r0 = pl.multiple_of(c * R, R)
        key = key_ref[pl.ds(r0, R), :]
        row = r0 + lax.broadcasted_iota(I32, (R, LANES), 0)
        tie_ok = jnp.where(row < jlim, 0.0, MASK_NEG)
        val = jnp.where(key > thr, 0.0, jnp.where(key == thr, tie_ok, MASK_NEG))
        bias_ref[pl.ds(r0, R), :] = jnp.where(row < adm_limit, val, MASK_NEG).astype(BF16)
        return carry

    lax.fori_loop(0, nch, bias_body, 0)

    n_kv = n_heads // GROUP
    cols = GROUP * QBLK
    m_ref[...] = jnp.full(m_ref.shape, -3e38, F32)
    acc_ref[...] = jnp.zeros(acc_ref.shape, F32)

    def scores(c, slot):
        r0 = pl.multiple_of(c * R, R)
        mask = bias_ref[pl.ds(r0, R), :]
        for g in range(n_kv):
            qg = q_ref[0, g * GROUP:(g + 1) * GROUP].reshape(cols, HEAD_DIM)
            kblk = k_ref[pl.ds(r0, R), g * HEAD_DIM:(g + 1) * HEAD_DIM]
            s = lax.dot_general(jnp.concatenate([kblk, mask], axis=1),
                                jnp.concatenate([qg, eye_ref[...]], axis=1),
                                NT_DIMS, preferred_element_type=F32)
            s_ref[slot, g] = s
            m_old = m_ref[1 - slot, g]
            m_new = jnp.maximum(m_old, jnp.max(s, axis=0, keepdims=True))
            alpha_ref[slot, g] = jnp.exp2(m_old - m_new)
            m_ref[slot, g] = m_new

    def accumulate(c, slot):
        for g in range(n_kv):
            p_ref[g] = jnp.exp2(s_ref[slot, g] - m_ref[slot, g][0:1]).astype(BF16)
        for g in range(n_kv):
            vt = vt_ref[c, g * V_ROWS:(g + 1) * V_ROWS, :]
            acc_ref[g] = alpha_ref[slot, g][0:1] * acc_ref[g] + jnp.dot(
                vt, p_ref[g], preferred_element_type=F32)

    scores(0, 0)

    def pair_body(pair, carry):
        c = 2 * pair
        scores(c + 1, 1)
        accumulate(c, 0)
        scores(c + 2, 0)
        accumulate(c + 1, 1)
        return carry

    lax.fori_loop(0, (nch - 1) // 2, pair_body, 0)

    @pl.when(nch % 2 == 0)
    def _():
        scores(nch - 1, 1)
        accumulate(nch - 2, 0)
        accumulate(nch - 1, 1)

    @pl.when(nch % 2 == 1)
    def _():
        accumulate(nch - 1, 0)

    for g in range(n_kv):
        acc = acc_ref[g]
        out_t = acc[0:HEAD_DIM] / acc[HEAD_DIM:HEAD_DIM + 1]
        for r in range(GROUP):
            h = g * GROUP + r
            o_ref[:, h * HEAD_DIM:(h + 1) * HEAD_DIM] = (
                out_t[:, r * QBLK:(r + 1) * QBLK].T.astype(o_ref.dtype))


def _dsa_attention(q, qi, wt, ki, k, vt, batch, seq):
    T = batch * seq
    nq = seq // QBLK
    n_heads = q.shape[1]
    kv_dim = k.shape[1]
    n_kv = n_heads // GROUP
    k_top = min(TOPK_MAX, seq // 4)
    cols = GROUP * QBLK
    vmem = (2 * (2 * n_heads * QBLK * HEAD_DIM * 2 + seq * IDX_DIM * 2 + 2 * seq * kv_dim * 2
                 + QBLK * n_heads * HEAD_DIM * 2)
            + 2 * seq * LANES * 4 + n_kv * (24 + HEAD_DIM) * cols * 4
            + IDX_ROWS * IDX_HEADS * QBLK * 4 + 4 * n_kv * cols * IDX_ROWS * 4)
    hm_spec = pl.BlockSpec((1, n_heads, QBLK, HEAD_DIM), lambda b, i: (b * nq + i, 0, 0, 0))
    return pl.pallas_call(
        functools.partial(_dsa_kernel, k_top=k_top),
        out_shape=jax.ShapeDtypeStruct((T, n_heads * HEAD_DIM), BF16),
        grid=(batch, nq),
        in_specs=[hm_spec, hm_spec,
                  pl.BlockSpec((IDX_HEADS, QBLK), lambda b, i: (0, b * nq + i)),
                  pl.BlockSpec((seq, IDX_DIM), lambda b, i: (b, 0)),
                  pl.BlockSpec((seq, kv_dim), lambda b, i: (b, 0)),
                  pl.BlockSpec((seq // IDX_ROWS, n_kv * V_ROWS, IDX_ROWS), lambda b, i: (b, 0, 0)),
                  pl.BlockSpec((cols, QBLK), lambda b, i: (0, 0))],
        out_specs=pl.BlockSpec((QBLK, n_heads * HEAD_DIM), lambda b, i: (b * nq + i, 0)),
        scratch_shapes=[pltpu.VMEM((seq, LANES), I32),
                        pltpu.VMEM((seq, LANES), BF16),
                        pltpu.VMEM((2, n_kv, 8, cols), F32),
                        pltpu.VMEM((2, n_kv, 8, cols), F32),
                        pltpu.VMEM((n_kv, V_ROWS, cols), F32),
                        pltpu.VMEM((2, n_kv, IDX_ROWS, cols), F32),
                        pltpu.VMEM((n_kv, IDX_ROWS, cols), BF16)],
        compiler_params=_params(2, vmem),
        name="dsa_attention",
    )(q, qi, wt, ki, k, vt, jnp.tile(jnp.eye(QBLK, dtype=BF16), (GROUP, 1)))


COL_CHUNK = 512


def _mm_res_kernel(a_ref, w_ref, b_ref, r_ref, o_ref):
    a = a_ref[...]
    for c in range(o_ref.shape[1] // COL_CHUNK):
        cs = slice(c * COL_CHUNK, (c + 1) * COL_CHUNK)
        y = jnp.dot(a, w_ref[:, cs], preferred_element_type=F32)
        o_ref[:, cs] = r_ref[:, cs] + (y + b_ref[:, cs])


def _mm_res(a, w_stack, layer, bias, res, tm=512):
    T, K = a.shape
    N = w_stack.shape[2]
    vmem = 2 * (tm * K * 2 + K * N * 2 + 2 * tm * N * 4) + 3 * tm * COL_CHUNK * 4
    return pl.pallas_call(
        _mm_res_kernel,
        out_shape=jax.ShapeDtypeStruct((T, N), F32),
        grid=(T // tm,),
        in_specs=[pl.BlockSpec((tm, K), lambda i: (i, 0)),
                  pl.BlockSpec((None, K, N), lambda i: (layer, 0, 0)),
                  pl.BlockSpec((1, N), lambda i: (0, 0)),
                  pl.BlockSpec((tm, N), lambda i: (i, 0))],
        out_specs=pl.BlockSpec((tm, N), lambda i: (i, 0)),
        compiler_params=_params(1, vmem),
        name="matmul_residual",
    )(a, w_stack, bias.reshape(1, N), res)


def _mlp_kernel(x_ref, g_ref, w1_ref, w2_ref, o_ref, hn_ref):
    f = pl.program_id(1)

    @pl.when(f == 0)
    def _():
        x = x_ref[...]
        ms = jnp.mean(x * x, axis=-1, keepdims=True)
        hn_ref[...] = (x * lax.rsqrt(ms + RMS_EPS) * g_ref[...]).astype(hn_ref.dtype)
        o_ref[...] = x

    a = jnp.maximum(jnp.dot(hn_ref[...], w1_ref[...], preferred_element_type=F32), 0.0)
    a = (a * a).astype(BF16)
    o_ref[...] += jnp.dot(a, w2_ref[...], preferred_element_type=F32)


def _mlp(x, g, w1_stack, w2_stack, layer, tm=512, tf=1024):
    T, D = x.shape
    F = w1_stack.shape[2]
    vmem = 2 * (2 * tm * D * 4 + 2 * D * tf * 2) + tm * D * 2 + tm * tf * 6 + tm * D * 4
    return pl.pallas_call(
        _mlp_kernel,
        out_shape=jax.ShapeDtypeStruct((T, D), F32),
        grid=(T // tm, F // tf),
        in_specs=[pl.BlockSpec((tm, D), lambda i, f: (i, 0)),
                  pl.BlockSpec((1, D), lambda i, f: (0, 0)),
                  pl.BlockSpec((None, D, tf), lambda i, f: (layer, 0, f)),
                  pl.BlockSpec((None, tf, D), lambda i, f: (layer, f, 0))],
        out_specs=pl.BlockSpec((tm, D), lambda i, f: (i, 0)),
        scratch_shapes=[pltpu.VMEM((tm, D), BF16)],
        compiler_params=_params(2, vmem),
        name="mlp",
    )(x, g.reshape(1, D), w1_stack, w2_stack)


def _ple_kernel(x_ref, p_ref, wg_ref, wp_ref, g_ref, *refs, emit_h):
    if emit_h:
        h_ref, y_ref = refs
    else:
        y_ref, h_ref = refs
    xb = x_ref[...].astype(BF16)
    pb = p_ref[...].astype(BF16)
    for c in range(h_ref.shape[1] // COL_CHUNK):
        cs = slice(c * COL_CHUNK, (c + 1) * COL_CHUNK)
        gate = jnp.dot(xb, wg_ref[:, cs], preferred_element_type=F32)
        emb = jnp.dot(pb, wp_ref[:, cs], preferred_element_type=F32)
        h_ref[:, cs] = x_ref[:, cs] + jax.nn.sigmoid(gate) * emb
    h = h_ref[...]
    ms = jnp.mean(h * h, axis=-1, keepdims=True)
    y_ref[...] = (h * lax.rsqrt(ms + RMS_EPS) * g_ref[...]).astype(y_ref.dtype)


def _ple(x, p_stack, wg_stack, wp_stack, layer, g, emit_h, y_dtype, tm=512):
    T, D = x.shape
    P = p_stack.shape[2]
    row = pl.BlockSpec((tm, D), lambda i: (i, 0))
    y_shape = jax.ShapeDtypeStruct((T, D), y_dtype)
    if emit_h:
        out_shape, out_specs, scratch = (jax.ShapeDtypeStruct((T, D), F32), y_shape), (row, row), []
    else:
        out_shape, out_specs, scratch = y_shape, row, [pltpu.VMEM((tm, D), F32)]
    vmem = (2 * (3 * tm * D * 4 + tm * P * 4 + D * D * 2 + P * D * 2)
            + tm * D * 2 + 3 * tm * COL_CHUNK * 4)
    return pl.pallas_call(
        functools.partial(_ple_kernel, emit_h=emit_h),
        out_shape=out_shape,
        grid=(T // tm,),
        in_specs=[row,
                  pl.BlockSpec((None, tm, P), lambda i: (layer, i, 0)),
                  pl.BlockSpec((None, D, D), lambda i: (layer, 0, 0)),
                  pl.BlockSpec((None, P, D), lambda i: (layer, 0, 0)),
                  pl.BlockSpec((1, D), lambda i: (0, 0))],
        out_specs=out_specs,
        scratch_shapes=scratch,
        compiler_params=_params(1, vmem),
        name="ple_norm",
    )(x, p_stack, wg_stack, wp_stack, g.reshape(1, D))


def _glu_kernel(a_ref, wa_ref, wg_ref, ba_ref, bg_ref, o_ref):
    a = a_ref[...]
    lin = jnp.dot(a, wa_ref[...].astype(BF16), preferred_element_type=F32) + ba_ref[...]
    gate = jnp.dot(a, wg_ref[...].astype(BF16), preferred_element_type=F32) + bg_ref[...]
    o_ref[...] = lin * jax.nn.sigmoid(gate)


def _glu(a, w_stack, layer, b, tm=1024, tn=512):
    T, K = a.shape
    N = w_stack.shape[2] // 2
    nj = N // tn
    vmem = 2 * (tm * K * 2 + 2 * K * tn * 4 + tm * tn * 4) + 2 * K * tn * 2 + 3 * tm * tn * 4
    return pl.pallas_call(
        _glu_kernel,
        out_shape=jax.ShapeDtypeStruct((T, N), F32),
        grid=(T // tm, nj),
        in_specs=[pl.BlockSpec((tm, K), lambda i, j: (i, 0)),
                  pl.BlockSpec((None, K, tn), lambda i, j: (layer, 0, j)),
                  pl.BlockSpec((None, K, tn), lambda i, j: (layer, 0, j + nj)),
                  pl.BlockSpec((1, tn), lambda i, j: (0, j)),
                  pl.BlockSpec((1, tn), lambda i, j: (0, j + nj))],
        out_specs=pl.BlockSpec((tm, tn), lambda i, j: (i, j)),
        compiler_params=_params(2, vmem),
        name="pw1_glu",
    )(a, w_stack, w_stack, b.reshape(1, 2 * N), b.reshape(1, 2 * N))


CONV_HALO = 32


def _conv_ln_kernel(u_ref, halo_ref, w_ref, b_ref, g_ref, beta_ref, o_ref, ubuf_ref, cbuf_ref,
                    sbuf_ref):
    n_chunks, tm, _ = cbuf_ref.shape
    at_start = pl.program_id(1) == 0
    for c in range(n_chunks):
        cols = slice(c * LANES, (c + 1) * LANES)
        ubuf_ref[c, 0:CONV_HALO, :] = jnp.where(at_start, 0.0, halo_ref[0, :, cols])
        ubuf_ref[c, CONV_HALO:, :] = u_ref[0, :, cols]

    first = CONV_HALO - (CONV_W - 1)

    def chunk_body(c, carry):
        acc = jnp.zeros((tm, LANES), F32)
        for phase in range(SUBLANES):
            taps = [k for k in range(CONV_W) if (first + k) % SUBLANES == phase]
            span = max(first + k for k in taps) - phase + tm
            if phase > 0:
                sbuf_ref[0:span, :] = ubuf_ref[c, phase:phase + span, :]
            for k in taps:
                off = first + k - phase
                rows = ubuf_ref[c, off:off + tm, :] if phase == 0 else sbuf_ref[off:off + tm, :]
                acc = acc + rows * w_ref[c, k:k + 1, :]
        cbuf_ref[c] = acc
        return carry

    lax.fori_loop(0, n_chunks, chunk_body, 0)
    y = jnp.concatenate([cbuf_ref[c] for c in range(n_chunks)], axis=1) + b_ref[...]
    mu = jnp.mean(y, axis=-1, keepdims=True)
    yc = y - mu
    var = jnp.mean(yc * yc, axis=-1, keepdims=True)
    yn = yc * lax.rsqrt(var + LN_EPS) * g_ref[...] + beta_ref[...]
    o_ref[0] = (yn * jax.nn.sigmoid(yn)).astype(o_ref.dtype)


def _conv_ln(u, w_dw, b_dw, ln_g, ln_b, tm=128):
    B, S, D = u.shape
    per = tm // CONV_HALO
    n_chunks = D // LANES
    w_pad = jnp.zeros((CONV_HALO, D), F32).at[:CONV_W].set(w_dw)
    w_chunks = w_pad.reshape(CONV_HALO, n_chunks, LANES).transpose(1, 0, 2)
    vec = pl.BlockSpec((1, D), lambda b, i: (0, 0))
    vmem = (2 * (tm * D * 4 + 2 * CONV_HALO * D * 4 + tm * D * 2)
            + (2 * tm + CONV_HALO) * D * 4 + (tm + CONV_HALO) * LANES * 4 + 4 * tm * D * 4)
    return pl.pallas_call(
        _conv_ln_kernel,
        out_shape=jax.ShapeDtypeStruct((B, S, D), BF16),
        grid=(B, S // tm),
        in_specs=[pl.BlockSpec((1, tm, D), lambda b, i: (b, i, 0)),
                  pl.BlockSpec((1, CONV_HALO, D), lambda b, i: (b, jnp.maximum(i * per - 1, 0), 0)),
                  pl.BlockSpec((n_chunks, CONV_HALO, LANES), lambda b, i: (0, 0, 0)),
                  vec, vec, vec],
        out_specs=pl.BlockSpec((1, tm, D), lambda b, i: (b, i, 0)),
        scratch_shapes=[pltpu.VMEM((n_chunks, tm + CONV_HALO, LANES), F32),
                        pltpu.VMEM((n_chunks, tm, LANES), F32),
                        pltpu.VMEM((tm + CONV_HALO, LANES), F32)],
        compiler_params=_params(2, vmem),
        name="dwconv_ln_silu",
    )(u, u, w_chunks, b_dw.reshape(1, D), ln_g.reshape(1, D), ln_b.reshape(1, D))


def kernel(x, p, positions, norm_mix_g, norm_mlp_g, final_g, a_w_in, a_w_out, a_kidx_g, a_kidx_b,
           b_w_pw1, b_b_pw1, b_w_dw, b_b_dw, b_ln_g, b_ln_b, b_w_pw2, b_b_pw2,
           mlp_w1, mlp_w2, ple_w_proj, ple_w_gate):
    B, S, D = x.shape
    T = B * S
    depth = norm_mix_g.shape[0]
    h = x.reshape(T, D)
    tabs = _rope_tables(positions.reshape(T, 1).astype(I32))

    q_dim = a_w_out.shape[1]
    idxq_dim = IDX_HEADS * IDX_DIM
    n_in = a_w_in.shape[2]
    kv_dim = (n_in - q_dim - idxq_dim - IDX_DIM - IDX_HEADS) // 2

    w1_bf = mlp_w1.astype(BF16)
    w2_bf = mlp_w2.astype(BF16)
    win_bf = jnp.pad(a_w_in.astype(BF16), ((0, 0), (0, 0), (0, -n_in % PROJ_CHUNK)))
    wout_bf = a_w_out.astype(BF16)
    wpw2_bf = b_w_pw2.astype(BF16)
    wgate_bf = ple_w_gate.astype(BF16)
    wproj_bf = ple_w_proj.astype(BF16)
    p_flat = p.reshape(depth, T, p.shape[-1])
    for i in range(depth):
        j = i // 2
        if i % 2 == 0:
            q, k, vt, qi, ki, wt = _in_proj(h, norm_mix_g[i], win_bf, j, tabs, a_kidx_g[j], a_kidx_b[j],
                                            q_dim, kv_dim, idxq_dim)
            attn = _dsa_attention(q, qi, wt, ki, k, vt, B, S)
            h = _mm_res(attn, wout_bf, j, jnp.zeros((D,), F32), h)
        else:
            u = _glu(hn, b_w_pw1, j, b_b_pw1[j])
            c = _conv_ln(u.reshape(B, S, D), b_w_dw[j], b_b_dw[j], b_ln_g[j], b_ln_b[j])
            h = _mm_res(c.reshape(T, D), wpw2_bf, j, b_b_pw2[j], h)
        h = _mlp(h, norm_mlp_g[i], w1_bf, w2_bf, i)
        if i + 1 < depth:
            h, hn = _ple(h, p_flat, wgate_bf, wproj_bf, i, norm_mix_g[i + 1], True, BF16)
        else:
            out = _ple(h, p_flat, wgate_bf, wproj_bf, i, final_g, False, F32)
    return out.reshape(B, S, D)
```

```python
import functools

import numpy as np
import jax
import jax.numpy as jnp
from jax import lax
from jax.experimental import pallas as pl
from jax.experimental.pallas import tpu as pltpu

F32 = jnp.float32
BF16 = jnp.bfloat16
I32 = jnp.int32

CHUNK = 64
QBLK = 128
HEAD_DIM = 128
GROUP = 4
IDX_HEADS = 16
IDX_DIM = 128
IDX_ROPE = 64
TOPK_MAX = 256
ROPE_THETA = 10000.0
CONV_W = 31
RMS_EPS = 1e-6
LN_EPS = 1e-5

LANES = 128
SUBLANES = 8
V7X_VMEM_BYTES = 64 * 1024 * 1024
VMEM_CAP = V7X_VMEM_BYTES - 8 * 1024 * 1024

INT_MIN = -2 ** 31
MASK_NEG = -1e30
LOG2_E = 1.4426950408889634
CHUNK_SHIFT = CHUNK.bit_length() - 1
NT_DIMS = (((1,), (1,)), ((), ()))


def _params(ndims, vmem_estimate):
    limit = min(int(vmem_estimate * 1.25) + (4 << 20), VMEM_CAP)
    return pltpu.CompilerParams(dimension_semantics=("arbitrary",) * ndims,
                                vmem_limit_bytes=limit)


def _rope_tables_kernel(pos_ref, inv_ref, sgn_ref, c128_ref, s128_ref, c64_ref, s64a_ref, s64b_ref):
    pos = pos_ref[...].astype(F32)
    ang = pos * inv_ref[0:1, :]
    c128_ref[...] = jnp.cos(ang)
    s128_ref[...] = jnp.sin(ang) * sgn_ref[0:1, :]
    ang = pos * inv_ref[1:2, :]
    c64_ref[...] = jnp.cos(ang)
    sn = jnp.sin(ang)
    s64a_ref[...] = sn * sgn_ref[1:2, :]
    s64b_ref[...] = sn * sgn_ref[2:3, :]


def _rope_tables(pos, tm=1024):
    T = pos.shape[0]
    half = HEAD_DIM // 2
    inv_h = ROPE_THETA ** (-jnp.arange(half, dtype=F32) * (2.0 / HEAD_DIM))
    half_i = IDX_ROPE // 2
    inv_i = ROPE_THETA ** (-jnp.arange(half_i, dtype=F32) * (2.0 / IDX_ROPE))
    inv = jnp.zeros((8, LANES), F32)
    inv = inv.at[0].set(jnp.concatenate([inv_h, inv_h]))
    inv = inv.at[1, :IDX_ROPE].set(jnp.concatenate([inv_i, inv_i]))
    sgn = np.zeros((8, LANES), np.float32)
    sgn[0, :half] = -1.0
    sgn[0, half:] = 1.0
    sgn[1, half_i:IDX_ROPE] = 1.0
    sgn[2, :half_i] = -1.0
    tab = jax.ShapeDtypeStruct((T, LANES), F32)
    row = pl.BlockSpec((tm, LANES), lambda i: (i, 0))
    cst = pl.BlockSpec((8, LANES), lambda i: (0, 0))
    return pl.pallas_call(
        _rope_tables_kernel,
        out_shape=(tab,) * 5,
        grid=(T // tm,),
        in_specs=[pl.BlockSpec((tm, 1), lambda i: (i, 0)), cst, cst],
        out_specs=(row,) * 5,
        compiler_params=_params(1, 16 * tm * LANES * 4),
        name="rope_tables",
    )(pos, inv, jnp.asarray(sgn))


PROJ_CHUNK = 512
IDX_ROWS = 256
V_ROWS = HEAD_DIM + 16
EXIT_CHECKS = (22, 25, 28)


def _rope_full(yh, c, s):
    return yh * c + pltpu.roll(yh, HEAD_DIM // 2, 1) * s


def _rope_partial(yh, c, sa, sb):
    return (yh * c + pltpu.roll(yh, IDX_ROPE // 2, 1) * sa
            + pltpu.roll(yh, IDX_DIM - IDX_ROPE // 2, 1) * sb)


def _in_proj_kernel(x_ref, g_ref, w_ref, c128_ref, s128_ref, c64_ref, s64a_ref, s64b_ref,
                    lng_ref, lnb_ref, q_ref, k_ref, vt_ref, qi_ref, ki_ref, wt_ref, hn_ref, y_ref,
                    *, n_q, n_qi, q_scale, w_scale):
    j = pl.program_id(1)
    tm = y_ref.shape[0]
    j_k, j_v, j_qi, j_ki = n_q, n_q + 1, n_q + 2, n_q + 2 + n_qi

    def multiply():
        y_ref[...] = jnp.dot(hn_ref[...], w_ref[...], preferred_element_type=F32)

    def head(h):
        return y_ref[:, h * HEAD_DIM:(h + 1) * HEAD_DIM]

    n_heads = PROJ_CHUNK // HEAD_DIM

    @pl.when(j == 0)
    def _():
        x = x_ref[...]
        ms = jnp.mean(x * x, axis=-1, keepdims=True)
        hn_ref[...] = (x * lax.rsqrt(ms + RMS_EPS) * g_ref[...]).astype(hn_ref.dtype)
        multiply()

    @pl.when((j >= 1) & (j <= j_k))
    def _():
        for h in range(n_heads):
            out = _rope_full(head(h), c128_ref[...], s128_ref[...]) * q_scale
            q_ref[:, h] = out.astype(q_ref.dtype).reshape(tm // QBLK, QBLK, HEAD_DIM)
        multiply()

    @pl.when(j == j_k + 1)
    def _():
        for h in range(n_heads):
            out = _rope_full(head(h), c128_ref[...], s128_ref[...])
            k_ref[:, h * HEAD_DIM:(h + 1) * HEAD_DIM] = out.astype(k_ref.dtype)
        multiply()

    @pl.when(j == j_v + 1)
    def _():
        yt = y_ref[...].T
        ones = jnp.ones((V_ROWS - HEAD_DIM, IDX_ROWS), vt_ref.dtype)
        for c in range(vt_ref.shape[0]):
            for g in range(PROJ_CHUNK // HEAD_DIM):
                vt_ref[c, g * V_ROWS:g * V_ROWS + HEAD_DIM, :] = yt[
                    g * HEAD_DIM:(g + 1) * HEAD_DIM, c * IDX_ROWS:(c + 1) * IDX_ROWS].astype(vt_ref.dtype)
                vt_ref[c, g * V_ROWS + HEAD_DIM:(g + 1) * V_ROWS, :] = ones
        multiply()

    @pl.when((j >= j_qi + 1) & (j <= j_ki))
    def _():
        for h in range(n_heads):
            out = _rope_partial(head(h), c64_ref[...], s64a_ref[...], s64b_ref[...])
            qi_ref[:, h] = out.astype(qi_ref.dtype).reshape(tm // QBLK, QBLK, HEAD_DIM)
        multiply()

    @pl.when(j == j_ki + 1)
    def _():
        yk = head(0)
        mu = jnp.mean(yk, axis=-1, keepdims=True)
        yc = yk - mu
        var = jnp.mean(yc * yc, axis=-1, keepdims=True)
        yn = yc * lax.rsqrt(var + LN_EPS) * lng_ref[...] + lnb_ref[...]
        ki_ref[...] = _rope_partial(yn, c64_ref[...], s64a_ref[...], s64b_ref[...]).astype(ki_ref.dtype)
        wt_ref[...] = head(1).T[0:IDX_HEADS, :] * w_scale


def _in_proj(x, g, w_stack, layer, tabs, ln_g, ln_b, q_dim, kv_dim, idxq_dim, tm=1024):
    T, D = x.shape
    assert q_dim % PROJ_CHUNK == 0 and idxq_dim % PROJ_CHUNK == 0 and kv_dim == PROJ_CHUNK
    assert IDX_DIM == HEAD_DIM and IDX_DIM + IDX_HEADS <= 2 * HEAD_DIM
    n_q, n_qi = q_dim // PROJ_CHUNK, idxq_dim // PROJ_CHUNK
    n_steps = n_q + 2 + n_qi + 1
    assert w_stack.shape[2] == n_steps * PROJ_CHUNK
    hpc = PROJ_CHUNK // HEAD_DIM
    tab_spec = pl.BlockSpec((tm, LANES), lambda i, j: (i, 0))
    vec_spec = pl.BlockSpec((1, IDX_DIM), lambda i, j: (0, 0))
    hm_block = (tm // QBLK, hpc, QBLK, HEAD_DIM)
    out_shape = (
        jax.ShapeDtypeStruct((T // QBLK, q_dim // HEAD_DIM, QBLK, HEAD_DIM), BF16),
        jax.ShapeDtypeStruct((T, kv_dim), BF16),
        jax.ShapeDtypeStruct((T // IDX_ROWS, kv_dim // HEAD_DIM * V_ROWS, IDX_ROWS), BF16),
        jax.ShapeDtypeStruct((T // QBLK, idxq_dim // HEAD_DIM, QBLK, HEAD_DIM), BF16),
        jax.ShapeDtypeStruct((T, IDX_DIM), BF16),
        jax.ShapeDtypeStruct((IDX_HEADS, T), F32))
    out_specs = (
        pl.BlockSpec(hm_block, lambda i, j: (i, jnp.clip(j - 1, 0, n_q - 1), 0, 0)),
        pl.BlockSpec((tm, kv_dim), lambda i, j: (i, 0)),
        pl.BlockSpec((tm // IDX_ROWS, kv_dim // HEAD_DIM * V_ROWS, IDX_ROWS), lambda i, j: (i, 0, 0)),
        pl.BlockSpec(hm_block, lambda i, j: (i, jnp.clip(j - 1 - (n_q + 2), 0, n_qi - 1), 0, 0)),
        pl.BlockSpec((tm, IDX_DIM), lambda i, j: (i, 0)),
        pl.BlockSpec((IDX_HEADS, tm), lambda i, j: (0, i)))
    vmem = (2 * (tm * D * 4 + D * PROJ_CHUNK * 2 + 5 * tm * LANES * 4 + 4 * tm * PROJ_CHUNK * 2)
            + tm * D * 2 + 6 * tm * PROJ_CHUNK * 4)
    return pl.pallas_call(
        functools.partial(_in_proj_kernel, n_q=n_q, n_qi=n_qi,
                          q_scale=HEAD_DIM ** -0.5 * LOG2_E,
                          w_scale=IDX_HEADS ** -0.5 * IDX_DIM ** -0.5),
        out_shape=out_shape,
        grid=(T // tm, n_steps + 1),
        in_specs=[pl.BlockSpec((tm, D), lambda i, j: (i, 0)),
                  pl.BlockSpec((1, D), lambda i, j: (0, 0)),
                  pl.BlockSpec((None, D, PROJ_CHUNK),
                               lambda i, j: (layer, 0, jnp.minimum(j, n_steps - 1))),
                  tab_spec, tab_spec, tab_spec, tab_spec, tab_spec, vec_spec, vec_spec],
        out_specs=out_specs,
        scratch_shapes=[pltpu.VMEM((tm, D), BF16), pltpu.VMEM((tm, PROJ_CHUNK), F32)],
        compiler_params=_params(2, vmem),
        name="dsa_in_proj",
    )(x, g.reshape(1, D), w_stack, *tabs, ln_g.reshape(1, IDX_DIM), ln_b.reshape(1, IDX_DIM))


def _sortable_key(x):
    b = lax.bitcast_convert_type(x, I32)
    return b ^ ((b >> 31) & 0x7FFFFFFF)


def _dsa_kernel(q_ref, qi_ref, wt_ref, ki_ref, k_ref, vt_ref, eye_ref, o_ref,
                key_ref, bias_ref, m_ref, alpha_ref, acc_ref, s_ref, p_ref,
                *, k_top):
    i = pl.program_id(1)
    seq = ki_ref.shape[0]
    n_heads = q_ref.shape[1]
    R = IDX_ROWS
    nch = (i * QBLK + QBLK + R - 1) // R

    q_glob = i * QBLK + lax.broadcasted_iota(I32, (1, LANES), 1)
    adm_limit = ((q_glob >> CHUNK_SHIFT) + 1) << CHUNK_SHIFT

    qi_all = qi_ref[0].reshape(IDX_HEADS * QBLK, IDX_DIM)

    def idx_chunk(c):
        r0 = pl.multiple_of(c * R, R)
        kib = ki_ref[pl.ds(r0, R), :]
        d = lax.dot_general(kib, qi_all, NT_DIMS, preferred_element_type=F32)
        acc = jnp.zeros((R, LANES), F32)
        for h in range(IDX_HEADS):
            acc = acc + jnp.maximum(d[:, h * QBLK:(h + 1) * QBLK], 0.0) * wt_ref[h:h + 1, :]
        row = r0 + lax.broadcasted_iota(I32, (R, LANES), 0)
        key_ref[pl.ds(r0, R), :] = jnp.where(row < adm_limit, _sortable_key(acc), INT_MIN)

    def idx_pair(pair, carry):
        idx_chunk(2 * pair)
        idx_chunk(2 * pair + 1)
        return carry

    lax.fori_loop(0, nch // 2, idx_pair, 0)

    nch2 = (nch + 1) // 2

    @pl.when(nch % 2 == 1)
    def _():
        idx_chunk(nch - 1)
        r0 = pl.multiple_of(nch * R, R)
        key_ref[pl.ds(r0, R), :] = jnp.full((R, LANES), INT_MIN, I32)

    def count(pred):
        def body(c, cnt):
            r0 = pl.multiple_of(c * R, R)
            blk = key_ref[pl.ds(r0, R), :]
            row = r0 + lax.broadcasted_iota(I32, (R, LANES), 0)
            return cnt + jnp.sum(pred(blk, row).reshape(R // 8, 8, LANES), axis=0)
        cnt = lax.fori_loop(0, nch, body, jnp.zeros((8, LANES), I32))
        return jnp.sum(cnt, axis=0, keepdims=True)

    def count_ge(cand):
        def body(c, cnt):
            r0 = pl.multiple_of(c * 2 * R, 2 * R)
            hit = jnp.where(key_ref[pl.ds(r0, 2 * R), :] >= cand, 1, 0)
            parts = [hit[t * 8:(t + 1) * 8, :] for t in range(2 * R // 8)]
            while len(parts) > 1:
                parts = [a + b for a, b in zip(parts[0::2], parts[1::2])]
            return cnt + parts[0]
        cnt = lax.fori_loop(0, nch2, body, jnp.zeros((8, LANES), I32))
        return jnp.sum(cnt, axis=0, keepdims=True)

    def thr_pass(b, state):
        lo, n_ge = state
        cand = lo + jnp.left_shift(jnp.int32(1), 31 - b)
        cnt = count_ge(cand)
        ok = cnt >= k_top
        return jnp.where(ok, cand, lo), jnp.where(ok, cnt, n_ge)

    def passes(first, last):
        return lambda state: lax.fori_loop(first, last, thr_pass, state)

    def settled(state):
        return jnp.max(jnp.where(state[1] == k_top, 0, 1)) == 0

    state = (jnp.full((1, LANES), INT_MIN, I32), jnp.full((1, LANES), 2 * R, I32) * nch2)
    state = passes(0, EXIT_CHECKS[0])(state)
    for first, last in zip(EXIT_CHECKS, EXIT_CHECKS[1:] + (32,)):
        state = lax.cond(settled(state), lambda s: s, passes(first, last), state)
    thr = state[0]

    def gt_eq_body(c, carry):
        r0 = pl.multiple_of(c * R, R)
        blk = key_ref[pl.ds(r0, R), :]
        gt = jnp.sum(jnp.where(blk > thr, 1, 0).reshape(R // 8, 8, LANES), axis=0)
        eq = jnp.sum(jnp.where(blk == thr, 1, 0).reshape(R // 8, 8, LANES), axis=0)
        return carry[0] + gt, carry[1] + eq

    zero = jnp.zeros((8, LANES), I32)
    c_gt, c_eq = [jnp.sum(v, axis=0, keepdims=True)
                  for v in lax.fori_loop(0, nch, gt_eq_body, (zero, zero))]
    need = k_top - c_gt
    n_bits = int(seq).bit_length()

    def tie_rows():
        def tie_pass(b, jlim):
            cand = jlim + jnp.left_shift(jnp.int32(1), n_bits - 1 - b)
            cnt = count(lambda key, row: jnp.where(key == thr, jnp.where(row < cand, 1, 0), 0))
            return jnp.where(cnt <= need, cand, jlim)
        return lax.fori_loop(0, n_bits, tie_pass, jnp.zeros((1, LANES), I32))

    has_excess_ties = jnp.max(jnp.where(c_eq > need, 1, 0)) > 0
    jlim = lax.cond(has_excess_ties, tie_rows, lambda: jnp.full((1, LANES), 2 * seq, I32))

    def bias_body(c, carry):
        r0 = pl.multiple_of(c * R, R)
        key = key_ref[pl.ds(r0, R), :]
        row = r0 + lax.broadcasted_iota(I32, (R, LANES), 0)
        tie_ok = jnp.where(row < jlim, 0.0, MASK_NEG)
        val = jnp.where(key > thr, 0.0, jnp.where(key == thr, tie_ok, MASK_NEG))
        bias_ref[pl.ds(r0, R), :] = jnp.where(row < adm_limit, val, MASK_NEG).astype(BF16)
        return carry

    lax.fori_loop(0, nch, bias_body, 0)

    n_kv = n_heads // GROUP
    cols = GROUP * QBLK
    m_ref[...] = jnp.full(m_ref.shape, -3e38, F32)
    acc_ref[...] = jnp.zeros(acc_ref.shape, F32)

    def scores(c, slot):
        r0 = pl.multiple_of(c * R, R)
        mask = bias_ref[pl.ds(r0, R), :]
        for g in range(n_kv):
            qg = q_ref[0, g * GROUP:(g + 1) * GROUP].reshape(cols, HEAD_DIM)
            kblk = k_ref[pl.ds(r0, R), g * HEAD_DIM:(g + 1) * HEAD_DIM]
            s = lax.dot_general(jnp.concatenate([kblk, mask], axis=1),
                                jnp.concatenate([qg, eye_ref[...]], axis=1),
                                NT_DIMS, preferred_element_type=F32)
            s_ref[slot, g] = s
            m_old = m_ref[1 - slot, g]
            m_new = jnp.maximum(m_old, jnp.max(s, axis=0, keepdims=True))
            alpha_ref[slot, g] = jnp.exp2(m_old - m_new)
            m_ref[slot, g] = m_new

    def accumulate(c, slot):
        for g in range(n_kv):
            p_ref[g] = jnp.exp2(s_ref[slot, g] - m_ref[slot, g][0:1]).astype(BF16)
        for g in range(n_kv):
            vt = vt_ref[c, g * V_ROWS:(g + 1) * V_ROWS, :]
            acc_ref[g] = alpha_ref[slot, g][0:1] * acc_ref[g] + jnp.dot(
                vt, p_ref[g], preferred_element_type=F32)

    scores(0, 0)

    def pair_body(pair, carry):
        c = 2 * pair
        scores(c + 1, 1)
        accumulate(c, 0)
        scores(c + 2, 0)
        accumulate(c + 1, 1)
        return carry

    lax.fori_loop(0, (nch - 1) // 2, pair_body, 0)

    @pl.when(nch % 2 == 0)
    def _():
        scores(nch - 1, 1)
        accumulate(nch - 2, 0)
        accumulate(nch - 1, 1)

    @pl.when(nch % 2 == 1)
    def _():
        accumulate(nch - 1, 0)

    for g in range(n_kv):
        acc = acc_ref[g]
        out_t = acc[0:HEAD_DIM] / acc[HEAD_DIM:HEAD_DIM + 1]
        for r in range(GROUP):
            h = g * GROUP + r
            o_ref[:, h * HEAD_DIM:(h + 1) * HEAD_DIM] = (
                out_t[:, r * QBLK:(r + 1) * QBLK].T.astype(o_ref.dtype))


def _dsa_attention(q, qi, wt, ki, k, vt, batch, seq):
    T = batch * seq
    nq = seq // QBLK
    n_heads = q.shape[1]
    kv_dim = k.shape[1]
    n_kv = n_heads // GROUP
    k_top = min(TOPK_MAX, seq // 4)
    cols = GROUP * QBLK
    vmem = (2 * (2 * n_heads * QBLK * HEAD_DIM * 2 + seq * IDX_DIM * 2 + 2 * seq * kv_dim * 2
                 + QBLK * n_heads * HEAD_DIM * 2)
            + 2 * seq * LANES * 4 + n_kv * (24 + HEAD_DIM) * cols * 4
            + IDX_ROWS * IDX_HEADS * QBLK * 4 + 4 * n_kv * cols * IDX_ROWS * 4)
    hm_spec = pl.BlockSpec((1, n_heads, QBLK, HEAD_DIM), lambda b, i: (b * nq + i, 0, 0, 0))
    return pl.pallas_call(
        functools.partial(_dsa_kernel, k_top=k_top),
        out_shape=jax.ShapeDtypeStruct((T, n_heads * HEAD_DIM), BF16),
        grid=(batch, nq),
        in_specs=[hm_spec, hm_spec,
                  pl.BlockSpec((IDX_HEADS, QBLK), lambda b, i: (0, b * nq + i)),
                  pl.BlockSpec((seq, IDX_DIM), lambda b, i: (b, 0)),
                  pl.BlockSpec((seq, kv_dim), lambda b, i: (b, 0)),
                  pl.BlockSpec((seq // IDX_ROWS, n_kv * V_ROWS, IDX_ROWS), lambda b, i: (b, 0, 0)),
                  pl.BlockSpec((cols, QBLK), lambda b, i: (0, 0))],
        out_specs=pl.BlockSpec((QBLK, n_heads * HEAD_DIM), lambda b, i: (b * nq + i, 0)),
        scratch_shapes=[pltpu.VMEM((seq, LANES), I32),
                        pltpu.VMEM((seq, LANES), BF16),
                        pltpu.VMEM((2, n_kv, 8, cols), F32),
                        pltpu.VMEM((2, n_kv, 8, cols), F32),
                        pltpu.VMEM((n_kv, V_ROWS, cols), F32),
                        pltpu.VMEM((2, n_kv, IDX_ROWS, cols), F32),
                        pltpu.VMEM((n_kv, IDX_ROWS, cols), BF16)],
        compiler_params=_params(2, vmem),
        name="dsa_attention",
    )(q, qi, wt, ki, k, vt, jnp.tile(jnp.eye(QBLK, dtype=BF16), (GROUP, 1)))


COL_CHUNK = 512


def _mm_res_kernel(a_ref, w_ref, b_ref, r_ref, o_ref):
    a = a_ref[...]
    for c in range(o_ref.shape[1] // COL_CHUNK):
        cs = slice(c * COL_CHUNK, (c + 1) * COL_CHUNK)
        y = jnp.dot(a, w_ref[:, cs], preferred_element_type=F32)
        o_ref[:, cs] = r_ref[:, cs] + (y + b_ref[:, cs])


def _mm_res(a, w_stack, layer, bias, res, tm=512):
    T, K = a.shape
    N = w_stack.shape[2]
    vmem = 2 * (tm * K * 2 + K * N * 2 + 2 * tm * N * 4) + 3 * tm * COL_CHUNK * 4
    return pl.pallas_call(
        _mm_res_kernel,
        out_shape=jax.ShapeDtypeStruct((T, N), F32),
        grid=(T // tm,),
        in_specs=[pl.BlockSpec((tm, K), lambda i: (i, 0)),
                  pl.BlockSpec((None, K, N), lambda i: (layer, 0, 0)),
                  pl.BlockSpec((1, N), lambda i: (0, 0)),
                  pl.BlockSpec((tm, N), lambda i: (i, 0))],
        out_specs=pl.BlockSpec((tm, N), lambda i: (i, 0)),
        compiler_params=_params(1, vmem),
        name="matmul_residual",
    )(a, w_stack, bias.reshape(1, N), res)


def _mlp_kernel(x_ref, g_ref, w1_ref, w2_ref, o_ref, hn_ref):
    f = pl.program_id(1)

    @pl.when(f == 0)
    def _():
        x = x_ref[...]
        ms = jnp.mean(x * x, axis=-1, keepdims=True)
        hn_ref[...] = (x * lax.rsqrt(ms + RMS_EPS) * g_ref[...]).astype(hn_ref.dtype)
        o_ref[...] = x

    a = jnp.maximum(jnp.dot(hn_ref[...], w1_ref[...], preferred_element_type=F32), 0.0)
    a = (a * a).astype(BF16)
    o_ref[...] += jnp.dot(a, w2_ref[...], preferred_element_type=F32)


def _mlp(x, g, w1_stack, w2_stack, layer, tm=512, tf=1024):
    T, D = x.shape
    F = w1_stack.shape[2]
    vmem = 2 * (2 * tm * D * 4 + 2 * D * tf * 2) + tm * D * 2 + tm * tf * 6 + tm * D * 4
    return pl.pallas_call(
        _mlp_kernel,
        out_shape=jax.ShapeDtypeStruct((T, D), F32),
        grid=(T // tm, F // tf),
        in_specs=[pl.BlockSpec((tm, D), lambda i, f: (i, 0)),
                  pl.BlockSpec((1, D), lambda i, f: (0, 0)),
                  pl.BlockSpec((None, D, tf), lambda i, f: (layer, 0, f)),
                  pl.BlockSpec((None, tf, D), lambda i, f: (layer, f, 0))],
        out_specs=pl.BlockSpec((tm, D), lambda i, f: (i, 0)),
        scratch_shapes=[pltpu.VMEM((tm, D), BF16)],
        compiler_params=_params(2, vmem),
        name="mlp",
    )(x, g.reshape(1, D), w1_stack, w2_stack)


def _ple_kernel(x_ref, p_ref, wg_ref, wp_ref, g_ref, *refs, emit_h):
    if emit_h:
        h_ref, y_ref = refs
    else:
        y_ref, h_ref = refs
    xb = x_ref[...].astype(BF16)
    pb = p_ref[...].astype(BF16)
    for c in range(h_ref.shape[1] // COL_CHUNK):
        cs = slice(c * COL_CHUNK, (c + 1) * COL_CHUNK)
        gate = jnp.dot(xb, wg_ref[:, cs], preferred_element_type=F32)
        emb = jnp.dot(pb, wp_ref[:, cs], preferred_element_type=F32)
        h_ref[:, cs] = x_ref[:, cs] + jax.nn.sigmoid(gate) * emb
    h = h_ref[...]
    ms = jnp.mean(h * h, axis=-1, keepdims=True)
    y_ref[...] = (h * lax.rsqrt(ms + RMS_EPS) * g_ref[...]).astype(y_ref.dtype)


def _ple(x, p_stack, wg_stack, wp_stack, layer, g, emit_h, y_dtype, tm=512):
    T, D = x.shape
    P = p_stack.shape[2]
    row = pl.BlockSpec((tm, D), lambda i: (i, 0))
    y_shape = jax.ShapeDtypeStruct((T, D), y_dtype)
    if emit_h:
        out_shape, out_specs, scratch = (jax.ShapeDtypeStruct((T, D), F32), y_shape), (row, row), []
    else:
        out_shape, out_specs, scratch = y_shape, row, [pltpu.VMEM((tm, D), F32)]
    vmem = (2 * (3 * tm * D * 4 + tm * P * 4 + D * D * 2 + P * D * 2)
            + tm * D * 2 + 3 * tm * COL_CHUNK * 4)
    return pl.pallas_call(
        functools.partial(_ple_kernel, emit_h=emit_h),
        out_shape=out_shape,
        grid=(T // tm,),
        in_specs=[row,
                  pl.BlockSpec((None, tm, P), lambda i: (layer, i, 0)),
                  pl.BlockSpec((None, D, D), lambda i: (layer, 0, 0)),
                  pl.BlockSpec((None, P, D), lambda i: (layer, 0, 0)),
                  pl.BlockSpec((1, D), lambda i: (0, 0))],
        out_specs=out_specs,
        scratch_shapes=scratch,
        compiler_params=_params(1, vmem),
        name="ple_norm",
    )(x, p_stack, wg_stack, wp_stack, g.reshape(1, D))


def _glu_kernel(a_ref, wa_ref, wg_ref, ba_ref, bg_ref, o_ref):
    a = a_ref[...]
    lin = jnp.dot(a, wa_ref[...].astype(BF16), preferred_element_type=F32) + ba_ref[...]
    gate = jnp.dot(a, wg_ref[...].astype(BF16), preferred_element_type=F32) + bg_ref[...]
    o_ref[...] = lin * jax.nn.sigmoid(gate)


def _glu(a, w_stack, layer, b, tm=1024, tn=512):
    T, K = a.shape
    N = w_stack.shape[2] // 2
    nj = N // tn
    vmem = 2 * (tm * K * 2 + 2 * K * tn * 4 + tm * tn * 4) + 2 * K * tn * 2 + 3 * tm * tn * 4
    return pl.pallas_call(
        _glu_kernel,
        out_shape=jax.ShapeDtypeStruct((T, N), F32),
        grid=(T // tm, nj),
        in_specs=[pl.BlockSpec((tm, K), lambda i, j: (i, 0)),
                  pl.BlockSpec((None, K, tn), lambda i, j: (layer, 0, j)),
                  pl.BlockSpec((None, K, tn), lambda i, j: (layer, 0, j + nj)),
                  pl.BlockSpec((1, tn), lambda i, j: (0, j)),
                  pl.BlockSpec((1, tn), lambda i, j: (0, j + nj))],
        out_specs=pl.BlockSpec((tm, tn), lambda i, j: (i, j)),
        compiler_params=_params(2, vmem),
        name="pw1_glu",
    )(a, w_stack, w_stack, b.reshape(1, 2 * N), b.reshape(1, 2 * N))


CONV_HALO = 32


def _conv_ln_kernel(u_ref, halo_ref, w_ref, b_ref, g_ref, beta_ref, o_ref, ubuf_ref, cbuf_ref,
                    sbuf_ref):
    n_chunks, tm, _ = cbuf_ref.shape
    at_start = pl.program_id(1) == 0
    for c in range(n_chunks):
        cols = slice(c * LANES, (c + 1) * LANES)
        ubuf_ref[c, 0:CONV_HALO, :] = jnp.where(at_start, 0.0, halo_ref[0, :, cols])
        ubuf_ref[c, CONV_HALO:, :] = u_ref[0, :, cols]

    first = CONV_HALO - (CONV_W - 1)

    def chunk_body(c, carry):
        acc = jnp.zeros((tm, LANES), F32)
        for phase in range(SUBLANES):
            taps = [k for k in range(CONV_W) if (first + k) % SUBLANES == phase]
            span = max(first + k for k in taps) - phase + tm
            if phase > 0:
                sbuf_ref[0:span, :] = ubuf_ref[c, phase:phase + span, :]
            for k in taps:
                off = first + k - phase
                rows = ubuf_ref[c, off:off + tm, :] if phase == 0 else sbuf_ref[off:off + tm, :]
                acc = acc + rows * w_ref[c, k:k + 1, :]
        cbuf_ref[c] = acc
        return carry

    lax.fori_loop(0, n_chunks, chunk_body, 0)
    y = jnp.concatenate([cbuf_ref[c] for c in range(n_chunks)], axis=1) + b_ref[...]
    mu = jnp.mean(y, axis=-1, keepdims=True)
    yc = y - mu
    var = jnp.mean(yc * yc, axis=-1, keepdims=True)
    yn = yc * lax.rsqrt(var + LN_EPS) * g_ref[...] + beta_ref[...]
    o_ref[0] = (yn * jax.nn.sigmoid(yn)).astype(o_ref.dtype)


def _conv_ln(u, w_dw, b_dw, ln_g, ln_b, tm=128):
    B, S, D = u.shape
    per = tm // CONV_HALO
    n_chunks = D // LANES
    w_pad = jnp.zeros((CONV_HALO, D), F32).at[:CONV_W].set(w_dw)
    w_chunks = w_pad.reshape(CONV_HALO, n_chunks, LANES).transpose(1, 0, 2)
    vec = pl.BlockSpec((1, D), lambda b, i: (0, 0))
    vmem = (2 * (tm * D * 4 + 2 * CONV_HALO * D * 4 + tm * D * 2)
            + (2 * tm + CONV_HALO) * D * 4 + (tm + CONV_HALO) * LANES * 4 + 4 * tm * D * 4)
    return pl.pallas_call(
        _conv_ln_kernel,
        out_shape=jax.ShapeDtypeStruct((B, S, D), BF16),
        grid=(B, S // tm),
        in_specs=[pl.BlockSpec((1, tm, D), lambda b, i: (b, i, 0)),
                  pl.BlockSpec((1, CONV_HALO, D), lambda b, i: (b, jnp.maximum(i * per - 1, 0), 0)),
                  pl.BlockSpec((n_chunks, CONV_HALO, LANES), lambda b, i: (0, 0, 0)),
                  vec, vec, vec],
        out_specs=pl.BlockSpec((1, tm, D), lambda b, i: (b, i, 0)),
        scratch_shapes=[pltpu.VMEM((n_chunks, tm + CONV_HALO, LANES), F32),
                        pltpu.VMEM((n_chunks, tm, LANES), F32),
                        pltpu.VMEM((tm + CONV_HALO, LANES), F32)],
        compiler_params=_params(2, vmem),
        name="dwconv_ln_silu",
    )(u, u, w_chunks, b_dw.reshape(1, D), ln_g.reshape(1, D), ln_b.reshape(1, D))


def kernel(x, p, positions, norm_mix_g, norm_mlp_g, final_g, a_w_in, a_w_out, a_kidx_g, a_kidx_b,
           b_w_pw1, b_b_pw1, b_w_dw, b_b_dw, b_ln_g, b_ln_b, b_w_pw2, b_b_pw2,
           mlp_w1, mlp_w2, ple_w_proj, ple_w_gate):
    B, S, D = x.shape
    T = B * S
    depth = norm_mix_g.shape[0]
    h = x.reshape(T, D)
    tabs = _rope_tables(positions.reshape(T, 1).astype(I32))

    q_dim = a_w_out.shape[1]
    idxq_dim = IDX_HEADS * IDX_DIM
    n_in = a_w_in.shape[2]
    kv_dim = (n_in - q_dim - idxq_dim - IDX_DIM - IDX_HEADS) // 2

    w1_bf = mlp_w1.astype(BF16)
    w2_bf = mlp_w2.astype(BF16)
    win_bf = jnp.pad(a_w_in.astype(BF16), ((0, 0), (0, 0), (0, -n_in % PROJ_CHUNK)))
    wout_bf = a_w_out.astype(BF16)
    wpw2_bf = b_w_pw2.astype(BF16)
    wgate_bf = ple_w_gate.astype(BF16)
    wproj_bf = ple_w_proj.astype(BF16)
    p_flat = p.reshape(depth, T, p.shape[-1])
    for i in range(depth):
        j = i // 2
        if i % 2 == 0:
            q, k, vt, qi, ki, wt = _in_proj(h, norm_mix_g[i], win_bf, j, tabs, a_kidx_g[j], a_kidx_b[j],
                                            q_dim, kv_dim, idxq_dim)
            attn = _dsa_attention(q, qi, wt, ki, k, vt, B, S)
            h = _mm_res(attn, wout_bf, j, jnp.zeros((D,), F32), h)
        else:
            u = _glu(hn, b_w_pw1, j, b_b_pw1[j])
            c = _conv_ln(u.reshape(B, S, D), b_w_dw[j], b_b_dw[j], b_ln_g[j], b_ln_b[j])
            h = _mm_res(c.reshape(T, D), wpw2_bf, j, b_b_pw2[j], h)
        h = _mlp(h, norm_mlp_g[i], w1_bf, w2_bf, i)
        if i + 1 < depth:
            h, hn = _ple(h, p_flat, wgate_bf, wproj_bf, i, norm_mix_g[i + 1], True, BF16)
        else:
            out = _ple(h, p_flat, wgate_bf, wproj_bf, i, final_g, False, F32)
    return out.reshape(B, S, D)
```
